```python
import jax, jax.numpy as jnp
from jax import lax
import numpy as np

D_MODEL = 2048
BATCH = 2
SEQ = 8192
DEPTH = 2

MIX_WIDTH = D_MODEL
DN_HEADS = 8
DN_HEAD_DIM = (MIX_WIDTH // 2) // DN_HEADS
DN_WIDTH = DN_HEADS * DN_HEAD_DIM
CONV_K = 4
CHUNK = 64
AT_HEAD_DIM = 64
AT_Q_HEADS = (MIX_WIDTH - DN_WIDTH) // AT_HEAD_DIM
AT_KV_HEADS = 2
AT_WIDTH = AT_Q_HEADS * AT_HEAD_DIM
AT_KV_WIDTH = AT_KV_HEADS * AT_HEAD_DIM
WINDOW = 128
ROPE_THETA = 10000.0
COL_SIZES = (3 * DN_WIDTH, DN_WIDTH, DN_HEADS, DN_HEADS, AT_WIDTH, AT_KV_WIDTH, AT_KV_WIDTH)
IN_COLS = 3 * DN_WIDTH + DN_WIDTH + 2 * DN_HEADS + AT_WIDTH + 2 * AT_KV_WIDTH
FFN_DIM = ((8 * D_MODEL + 3 * 256 - 1) // (3 * 256)) * 256
N_MOD = 6
EPS = 1e-6

kernel_name = "hybrid_deltanet_swa_sink_adaln_block"


def rms_norm(x, gain):
    xf = x.astype(jnp.float32)
    y = xf * lax.rsqrt(jnp.mean(xf * xf, axis=-1, keepdims=True) + EPS)
    return (y * gain.astype(jnp.float32)).astype(x.dtype)


def l2_norm(x):
    return x * lax.rsqrt(jnp.sum(x * x, axis=-1, keepdims=True) + EPS)


def rope(x, pos):
    d = x.shape[-1]
    half = d // 2
    inv_freq = ROPE_THETA ** (-jnp.arange(half, dtype=jnp.float32) * 2.0 / d)
    ang = pos.astype(jnp.float32)[:, None] * inv_freq[None, :]
    cos = jnp.cos(ang)[None, :, None, :]
    sin = jnp.sin(ang)[None, :, None, :]
    xf = x.astype(jnp.float32)
    x1, x2 = xf[..., :half], xf[..., half:]
    return jnp.concatenate([x1 * cos - x2 * sin, x2 * cos + x1 * sin], axis=-1).astype(x.dtype)


def to_chunks(t):
    b, tl, h = t.shape[:3]
    t = t.reshape(b, tl // CHUNK, CHUNK, h, *t.shape[3:])
    return jnp.moveaxis(t, 3, 1)


def chunk_gated_delta_rule(q, k, v, g, beta):
    b, tl, h, dv = v.shape
    qc, kc, vc = to_chunks(q), to_chunks(k), to_chunks(v)
    gc = jnp.cumsum(to_chunks(g), axis=-1)
    bc = to_chunks(beta)[..., None]
    causal = jnp.tril(jnp.ones((CHUNK, CHUNK), dtype=bool))
    strict = jnp.tril(jnp.ones((CHUNK, CHUNK), dtype=bool), -1)
    decay = jnp.exp(jnp.where(causal, gc[..., :, None] - gc[..., None, :], -jnp.inf))
    kb = kc * bc
    vb = vc * bc
    eye = jnp.eye(CHUNK, dtype=jnp.float32)
    lower = jnp.where(strict, jnp.einsum('bhncd,bhnsd->bhncs', kb, kc) * decay, 0.0) + eye
    t_inv = lax.linalg.triangular_solve(lower, jnp.broadcast_to(eye, lower.shape),
                                        left_side=True, lower=True, unit_diagonal=True)
    w = jnp.einsum('bhncs,bhnsd->bhncd', t_inv, kb * jnp.exp(gc)[..., None])
    u = jnp.einsum('bhncs,bhnsd->bhncd', t_inv, vb)
    intra = jnp.where(causal, jnp.einsum('bhncd,bhnsd->bhncs', qc, kc) * decay, 0.0)
    qg = qc * jnp.exp(gc)[..., None]
    kd = kc * jnp.exp(gc[..., -1:] - gc)[..., None]
    glast = jnp.exp(gc[..., -1])

    def step(state, inp):
        w_i, u_i, qg_i, intra_i, kd_i, gl_i = inp
        v_new = u_i - jnp.einsum('bhck,bhkv->bhcv', w_i, state)
        o_i = jnp.einsum('bhck,bhkv->bhcv', qg_i, state) + jnp.einsum('bhcs,bhsv->bhcv', intra_i, v_new)
        state = state * gl_i[..., None, None] + jnp.einsum('bhck,bhcv->bhkv', kd_i, v_new)
        return state, o_i

    seq_first = lambda t: jnp.moveaxis(t, 2, 0)
    state0 = jnp.zeros((b, h, q.shape[-1], dv), jnp.float32)
    _, o = lax.scan(step, state0, (seq_first(w), seq_first(u), seq_first(qg), seq_first(intra),
                                   seq_first(kd), jnp.moveaxis(glast, 2, 0)))
    o = jnp.moveaxis(jnp.moveaxis(o, 0, 2), 1, 3)
    return o.reshape(b, tl, h, dv)


def gated_deltanet(qkv, z, b_raw, a_raw, conv_w, a_log, dt_bias, norm_w):
    bsz, tl, _ = qkv.shape
    conv = lax.conv_general_dilated(qkv, conv_w[:, None, :].astype(qkv.dtype), window_strides=(1,),
                                    padding=[(CONV_K - 1, 0)], dimension_numbers=('NWC', 'WIO', 'NWC'),
                                    feature_group_count=3 * DN_WIDTH)
    conv = jax.nn.silu(conv.astype(jnp.float32))
    q, k, v = jnp.split(conv, 3, axis=-1)
    shp = (bsz, tl, DN_HEADS, DN_HEAD_DIM)
    q = l2_norm(q.reshape(shp)) * (DN_HEAD_DIM ** -0.5)
    k = l2_norm(k.reshape(shp))
    v = v.reshape(shp)
    beta = jax.nn.sigmoid(b_raw.astype(jnp.float32))
    g = -jnp.exp(a_log.astype(jnp.float32)) * jax.nn.softplus(a_raw.astype(jnp.float32) + dt_bias.astype(jnp.float32))
    o = chunk_gated_delta_rule(q, k, v, g, beta)
    o = o * lax.rsqrt(jnp.mean(o * o, axis=-1, keepdims=True) + EPS) * norm_w.astype(jnp.float32)
    o = o * jax.nn.silu(z.astype(jnp.float32).reshape(shp))
    return o.reshape(bsz, tl, DN_WIDTH).astype(qkv.dtype)


def sliding_window_attention_sinks(q, k, v, sinks):
    bsz, tl, _, dh = q.shape
    nb = tl // WINDOW
    grp = AT_Q_HEADS // AT_KV_HEADS
    qb = q.reshape(bsz, nb, WINDOW, AT_KV_HEADS, grp, dh)
    def band(t):
        tb = t.reshape(bsz, nb, WINDOW, AT_KV_HEADS, dh)
        prev = jnp.pad(tb, ((0, 0), (1, 0), (0, 0), (0, 0), (0, 0)))[:, :-1]
        return jnp.concatenate([prev, tb], axis=2)
    kw, vw = band(k), band(v)
    s = jnp.einsum('bnqhgd,bnkhd->bnhgqk', qb, kw).astype(jnp.float32) * (dh ** -0.5)
    r = jnp.arange(WINDOW)[:, None]
    j = jnp.arange(2 * WINDOW)[None, :]
    in_band = (j > r) & (j <= r + WINDOW)
    valid = (jnp.arange(nb)[:, None, None] > 0) | (j >= WINDOW)[None]
    mask = (in_band[None] & valid)[None, :, None, None]
    s = jnp.where(mask, s, -jnp.inf)
    sink = sinks.astype(jnp.float32).reshape(AT_KV_HEADS, grp)[None, None, :, :, None, None]
    m = jnp.maximum(jnp.max(s, axis=-1, keepdims=True), sink)
    p = jnp.exp(s - m)
    p = p / (jnp.sum(p, axis=-1, keepdims=True) + jnp.exp(sink - m))
    o = jnp.einsum('bnhgqk,bnkhd->bnqhgd', p.astype(v.dtype), vw)
    return o.reshape(bsz, tl, AT_WIDTH)


def setup_inputs(seed: int = 0) -> dict:
    key = jax.random.key(seed)
    ks = jax.random.split(key, 20)
    f32 = jnp.float32
    nrm = lambda k, shp, s: jax.random.normal(k, shp, f32) * s
    return {
        "x": nrm(ks[0], (BATCH, SEQ, D_MODEL), 1.0),
        "c": nrm(ks[1], (BATCH, D_MODEL), 1.0),
        "ln_mix": 1.0 + nrm(ks[2], (DEPTH, D_MODEL), 0.02),
        "ln_ffn": 1.0 + nrm(ks[3], (DEPTH, D_MODEL), 0.02),
        "w_ada": nrm(ks[4], (DEPTH, D_MODEL, N_MOD * D_MODEL), D_MODEL ** -0.5),
        "b_ada": nrm(ks[5], (DEPTH, N_MOD * D_MODEL), 0.02),
        "w_in": nrm(ks[6], (DEPTH, D_MODEL, IN_COLS), D_MODEL ** -0.5),
        "dn_conv_w": nrm(ks[7], (DEPTH, CONV_K, 3 * DN_WIDTH), CONV_K ** -0.5),
        "dn_a_log": jnp.log(jax.random.uniform(ks[8], (DEPTH, DN_HEADS), f32, 1.0, 16.0)),
        "dn_dt_bias": jnp.log(jnp.expm1(jax.random.uniform(ks[9], (DEPTH, DN_HEADS), f32, 0.001, 0.1))),
        "dn_norm_w": 1.0 + nrm(ks[10], (DEPTH, DN_HEAD_DIM), 0.02),
        "attn_sinks": nrm(ks[11], (DEPTH, AT_Q_HEADS), 0.5),
        "w_out": nrm(ks[12], (DEPTH, MIX_WIDTH, D_MODEL), MIX_WIDTH ** -0.5),
        "w_gate_up": nrm(ks[13], (DEPTH, D_MODEL, 2 * FFN_DIM), D_MODEL ** -0.5),
        "w_down": nrm(ks[14], (DEPTH, FFN_DIM, D_MODEL), FFN_DIM ** -0.5),
        "ln_final": 1.0 + nrm(ks[15], (D_MODEL,), 0.02),
    }


def reference(x, c, ln_mix, ln_ffn, w_ada, b_ada, w_in, dn_conv_w, dn_a_log, dn_dt_bias,
              dn_norm_w, attn_sinks, w_out, w_gate_up, w_down, ln_final):
    bsz, tl, _ = x.shape
    pos = jnp.arange(tl, dtype=jnp.int32)
    split_at = np.cumsum(COL_SIZES)[:-1].tolist()
    c_act = jax.nn.silu(c)
    for l in range(DEPTH):
        mod = c_act @ w_ada[l] + b_ada[l]
        sh_m, sc_m, gt_m, sh_f, sc_f, gt_f = [t[:, None, :] for t in jnp.split(mod, N_MOD, axis=-1)]
        h = rms_norm(x, ln_mix[l]) * (1.0 + sc_m) + sh_m
        proj = h @ w_in[l]
        dn_qkv, dn_z, dn_b, dn_a, at_q, at_k, at_v = jnp.split(proj, split_at, axis=-1)
        dn_out = gated_deltanet(dn_qkv, dn_z, dn_b, dn_a, dn_conv_w[l], dn_a_log[l],
                                dn_dt_bias[l], dn_norm_w[l])
        q = rope(at_q.reshape(bsz, tl, AT_Q_HEADS, AT_HEAD_DIM), pos)
        k = rope(at_k.reshape(bsz, tl, AT_KV_HEADS, AT_HEAD_DIM), pos)
        v = at_v.reshape(bsz, tl, AT_KV_HEADS, AT_HEAD_DIM)
        at_out = sliding_window_attention_sinks(q, k, v, attn_sinks[l])
        mix = jnp.concatenate([dn_out, at_out], axis=-1) @ w_out[l]
        x = x + gt_m * mix
        h = rms_norm(x, ln_ffn[l]) * (1.0 + sc_f) + sh_f
        gate, up = jnp.split(h @ w_gate_up[l], 2, axis=-1)
        x = x + gt_f * ((jax.nn.silu(gate) * up) @ w_down[l])
    return rms_norm(x, ln_final)
```

```python
import functools

import jax
import jax.numpy as jnp
import numpy as np
from jax import lax
from jax.experimental import pallas as pl
from jax.experimental.pallas import tpu as pltpu

F32 = jnp.float32
BF16 = jnp.bfloat16

AT_HEAD_DIM = 64
AT_KV_HEADS = 2
CHUNK = 64
WINDOW = 128
ROPE_THETA = 10000.0
EPS = 1e-6
N_MOD = 6

V7X_LANES = 128
V7X_SUBLANES = 8
V7X_VMEM_BYTES = 64 * 1024 * 1024
VMEM_REQUEST_CAP = 60000 * 1024


def _cparams(semantics, vmem_estimate_bytes):
    limit = min(max(int(vmem_estimate_bytes * 1.25), 16 * 1024 * 1024), VMEM_REQUEST_CAP)
    return pltpu.CompilerParams(dimension_semantics=semantics, vmem_limit_bytes=limit)


def _dot(a, b):
    return jnp.dot(a, b, preferred_element_type=F32)


def _dot_nt(a, b):
    return lax.dot_general(a, b, (((1,), (1,)), ((), ())), preferred_element_type=F32)


def _dot_tn(a, b):
    return lax.dot_general(a, b, (((0,), (0,)), ((), ())), preferred_element_type=F32)


def _split3(x):
    x1 = x.astype(BF16)
    r1 = x - x1.astype(F32)
    x2 = r1.astype(BF16)
    x3 = (r1 - x2.astype(F32)).astype(BF16)
    return x1, x2, x3


def _dot_sel(sel_bf16, x):
    x1, x2, x3 = _split3(x)
    return _dot(sel_bf16, x1) + _dot(sel_bf16, x2) + _dot(sel_bf16, x3)


def _div_pow2(x, n):
    shift = n.bit_length() - 1
    assert n == 1 << shift
    return lax.shift_right_arithmetic(x, shift)


def _silu(x):
    return x * jax.nn.sigmoid(x)


def _softplus(x):
    return jnp.maximum(x, 0.0) + jnp.log1p(jnp.exp(-jnp.abs(x)))


def _norm_mod(x, gain, shift, scale):
    var = jnp.mean(x * x, axis=-1, keepdims=True)
    y = x * lax.rsqrt(var + EPS) * gain
    return y * (1.0 + scale) + shift


def _adaln_kernel(c_ref, w_ref, b_ref, o_ref):
    ca = _silu(c_ref[...]).astype(BF16)
    o_ref[...] = _dot(ca, w_ref[...].astype(BF16)) + b_ref[...]


def _adaln(c_pad, w_ada, b_ada, tn=1024):
    depth, d, n = w_ada.shape
    rows = c_pad.shape[0]
    est = 2 * d * tn * 4 + d * tn * 2 + 4 * rows * (d + tn) * 4
    return pl.pallas_call(
        _adaln_kernel,
        grid=(depth, n // tn),
        in_specs=[
            pl.BlockSpec((rows, d), lambda l, j: (0, 0)),
            pl.BlockSpec((None, d, tn), lambda l, j: (l, 0, j)),
            pl.BlockSpec((None, 1, tn), lambda l, j: (l, 0, j)),
        ],
        out_specs=pl.BlockSpec((None, rows, tn), lambda l, j: (l, 0, j)),
        out_shape=jax.ShapeDtypeStruct((depth, rows, n), F32),
        compiler_params=_cparams(("arbitrary", "arbitrary"), est),
        name="adaln_mod",
    )(c_pad, w_ada, b_ada.reshape(depth, 1, n))


def _in_proj_kernel(x_ref, g_ref, mod_ref, w_ref, o_ref, h_ref):
    @pl.when(pl.program_id(1) == 0)
    def _():
        h = _norm_mod(x_ref[...], g_ref[...], mod_ref[0:1, :], mod_ref[1:2, :])
        h_ref[...] = h.astype(BF16)

    o_ref[...] = _dot(h_ref[...], w_ref[...])


def _in_proj(x2, gain, mod_l, w_bf16, tokens_per_batch, tm=1024, tn=512):
    n, d = x2.shape
    ncols = w_bf16.shape[1]
    tm = min(tm, tokens_per_batch)
    tiles_per_batch = tokens_per_batch // tm
    est = 2 * tm * d * 4 + tm * d * 2 + 2 * d * tn * 2 + 2 * tm * tn * 4 + 2 * tm * d * 4
    return pl.pallas_call(
        _in_proj_kernel,
        grid=(n // tm, ncols // tn),
        in_specs=[
            pl.BlockSpec((tm, d), lambda i, j: (i, 0)),
            pl.BlockSpec((1, d), lambda i, j: (0, 0)),
            pl.BlockSpec((None, N_MOD, d), lambda i, j: (i // tiles_per_batch, 0, 0)),
            pl.BlockSpec((d, tn), lambda i, j: (0, j)),
        ],
        out_specs=pl.BlockSpec((tm, tn), lambda i, j: (i, j)),
        out_shape=jax.ShapeDtypeStruct((n, ncols), F32),
        scratch_shapes=[pltpu.VMEM((tm, d), BF16)],
        compiler_params=_cparams(("parallel", "arbitrary"), est),
        name="in_proj",
    )(x2, gain.reshape(1, d), mod_l, w_bf16)


def _tri_inverse(a, eye, m_diag16, m_off32, m_off64):
    b = jnp.where(m_diag16, a, 0.0)
    p = eye - b
    for _ in range(3):
        bb = b.astype(BF16)
        b = _dot(bb, bb)
        p = p + _dot(p.astype(BF16), b.astype(BF16))
    for m in (m_off32, m_off64):
        pb = p.astype(BF16)
        t = _dot(jnp.where(m, a, 0.0).astype(BF16), pb)
        p = p - _dot(pb, t.astype(BF16))
    return p


def _dn_kernel(q_ref, k_ref, v_ref, z_ref, ba_ref, cw_ref, alog_ref, dtb_ref, nw_ref, o_ref,
               ext_ref, tail_ref, qs_ref, ks_ref, vs_ref, s_ref, *, tb, heads, hd, conv_k):
    width = heads * hd
    t = pl.program_id(1)

    @pl.when(t == 0)
    def _():
        tail_ref[...] = jnp.zeros_like(tail_ref)
        s_ref[...] = jnp.zeros_like(s_ref)

    for seg, (src, dst) in enumerate(((q_ref, qs_ref), (k_ref, ks_ref), (v_ref, vs_ref))):
        cur = src[...]
        ext_ref[0:V7X_SUBLANES, :] = tail_ref[seg]
        ext_ref[V7X_SUBLANES:, :] = cur
        lo, hi = seg * width, (seg + 1) * width
        acc = cur * cw_ref[conv_k - 1:conv_k, lo:hi]
        for j in range(1, conv_k):
            acc = acc + ext_ref[pl.ds(V7X_SUBLANES - j, tb), :] * cw_ref[conv_k - 1 - j:conv_k - j, lo:hi]
        tail_ref[seg] = cur[tb - V7X_SUBLANES:, :]
        y = _silu(acc)
        if seg == 2:
            dst[...] = y
        else:
            post = hd ** -0.5 if seg == 0 else 1.0
            for h in range(heads):
                yh = y[:, h * hd:(h + 1) * hd]
                inv = lax.rsqrt(jnp.sum(yh * yh, axis=-1, keepdims=True) + EPS)
                dst[:, h * hd:(h + 1) * hd] = yh * (inv * post) if post != 1.0 else yh * inv

    ba = ba_ref[...]
    beta_all = jax.nn.sigmoid(ba)
    g_all = -jnp.exp(alog_ref[...]) * _softplus(ba + dtb_ref[...])

    ri = lax.broadcasted_iota(jnp.int32, (tb, tb), 0)
    ci = lax.broadcasted_iota(jnp.int32, (tb, tb), 1)
    same_chunk = _div_pow2(ri, CHUNK) == _div_pow2(ci, CHUNK)
    cum_sel = jnp.where(same_chunk & (ci <= ri), 1.0, 0.0).astype(BF16)
    tot_sel = jnp.where(same_chunk, 1.0, 0.0).astype(BF16)
    gc_all = _dot_sel(cum_sel, g_all)
    gtot_all = _dot_sel(tot_sel, g_all)
    gc_t = gc_all.T

    ii = lax.broadcasted_iota(jnp.int32, (CHUNK, CHUNK), 0)
    jj = lax.broadcasted_iota(jnp.int32, (CHUNK, CHUNK), 1)
    lower = ii >= jj
    strict = ii > jj
    eye = jnp.where(ii == jj, 1.0, 0.0).astype(F32)
    i16, j16, i32, j32 = _div_pow2(ii, 16), _div_pow2(jj, 16), _div_pow2(ii, 32), _div_pow2(jj, 32)
    m_diag16 = i16 == j16
    m_off32 = (i32 == j32) & (i16 > j16)
    m_off64 = i32 > j32
    nw = nw_ref[...]

    for c in range(tb // CHUNK):
        r = slice(c * CHUNK, (c + 1) * CHUNK)
        for h in range(heads):
            hs = slice(h * hd, (h + 1) * hd)
            gcol = gc_all[r, heads + h:heads + h + 1]
            grow = gc_t[heads + h:heads + h + 1, r]
            gtot = gtot_all[r, heads + h:heads + h + 1]
            beta = beta_all[r, h:h + 1]
            decay = jnp.exp(jnp.where(lower, gcol - grow, -jnp.inf))
            egc = jnp.exp(gcol)
            q = qs_ref[r, hs]
            k = ks_ref[r, hs]
            v = vs_ref[r, hs]
            kb = k * beta
            kq = _dot_nt(jnp.concatenate([kb, q], axis=0).astype(BF16), k.astype(BF16))
            a = jnp.where(strict, kq[:CHUNK] * decay, 0.0)
            intra = kq[CHUNK:] * decay
            tinv = _tri_inverse(a, eye, m_diag16, m_off32, m_off64)
            rhs = jnp.concatenate([kb * egc, v * beta], axis=1).astype(BF16)
            wu = _dot(tinv.astype(BF16), rhs)
            state = s_ref[h]
            ws = _dot(jnp.concatenate([wu[:, :hd], q * egc], axis=0).astype(BF16), state.astype(BF16))
            v_new = wu[:, hd:] - ws[:CHUNK]
            v_new_b = v_new.astype(BF16)
            o = ws[CHUNK:] + _dot(intra.astype(BF16), v_new_b)
            kd = k * jnp.exp(gtot - gcol)
            s_ref[h] = state * jnp.exp(gtot[0:1, :]) + _dot_tn(kd.astype(BF16), v_new_b)
            on = o * lax.rsqrt(jnp.mean(o * o, axis=-1, keepdims=True) + EPS) * nw
            o_ref[r, hs] = (on * _silu(z_ref[r, hs])).astype(o_ref.dtype)


def _deltanet(proj, conv_w, alog_row, dtb_row, norm_w, batch, seq, heads, hd, ba_blk, tb=128):
    width = heads * hd
    conv_k = conv_w.shape[0]
    assert conv_k - 1 <= V7X_SUBLANES and tb % CHUNK == 0 and seq % tb == 0
    nt = seq // tb

    def col(cb):
        return lambda b, t: (b * nt + t, cb)

    est = (2 * 4 * tb * width * 4 + 2 * tb * width * 2 + (tb + V7X_SUBLANES) * width * 4
           + 3 * V7X_SUBLANES * width * 4 + 3 * tb * width * 4 + heads * hd * hd * 4
           + 2 * conv_k * 3 * width * 4 + 8 * tb * tb * 4)
    kern = functools.partial(_dn_kernel, tb=tb, heads=heads, hd=hd, conv_k=conv_k)
    return pl.pallas_call(
        kern,
        grid=(batch, nt),
        in_specs=[
            pl.BlockSpec((tb, width), col(0)),
            pl.BlockSpec((tb, width), col(1)),
            pl.BlockSpec((tb, width), col(2)),
            pl.BlockSpec((tb, width), col(3)),
            pl.BlockSpec((tb, V7X_LANES), col(ba_blk)),
            pl.BlockSpec((conv_k, 3 * width), lambda b, t: (0, 0)),
            pl.BlockSpec((1, V7X_LANES), lambda b, t: (0, 0)),
            pl.BlockSpec((1, V7X_LANES), lambda b, t: (0, 0)),
            pl.BlockSpec((1, hd), lambda b, t: (0, 0)),
        ],
        out_specs=pl.BlockSpec((tb, width), lambda b, t: (b * nt + t, 0)),
        out_shape=jax.ShapeDtypeStruct((batch * seq, width), BF16),
        scratch_shapes=[
            pltpu.VMEM((tb + V7X_SUBLANES, width), F32),
            pltpu.VMEM((3, V7X_SUBLANES, width), F32),
            pltpu.VMEM((tb, width), F32),
            pltpu.VMEM((tb, width), F32),
            pltpu.VMEM((tb, width), F32),
            pltpu.VMEM((heads, hd, hd), F32),
        ],
        compiler_params=_cparams(("parallel", "arbitrary"), est),
        name="gated_deltanet",
    )(proj, proj, proj, proj, proj, conv_w, alog_row, dtb_row, norm_w.reshape(1, hd))


def _rope_kernel(f_ref, cos_ref, sin_ref, *, tr):
    pos = lax.broadcasted_iota(jnp.int32, (tr, V7X_LANES), 0) + pl.program_id(0) * tr
    ang = pos.astype(F32) * f_ref[...]
    lane = lax.broadcasted_iota(jnp.int32, (tr, V7X_LANES), 1)
    cos_ref[...] = jnp.cos(ang)
    sin_ref[...] = jnp.where(lane < V7X_LANES // 2, -jnp.sin(ang), jnp.sin(ang))


def _rope_tables(seq, tr=512):
    half = AT_HEAD_DIM // 2
    inv_freq = ROPE_THETA ** (-jnp.arange(half, dtype=F32) * 2.0 / AT_HEAD_DIM)
    f_row = jnp.tile(inv_freq, V7X_LANES // half).reshape(1, V7X_LANES)
    tr = min(tr, seq)
    return pl.pallas_call(
        functools.partial(_rope_kernel, tr=tr),
        grid=(seq // tr,),
        in_specs=[pl.BlockSpec((1, V7X_LANES), lambda i: (0, 0))],
        out_specs=[pl.BlockSpec((tr, V7X_LANES), lambda i: (i, 0))] * 2,
        out_shape=[jax.ShapeDtypeStruct((seq, V7X_LANES), F32)] * 2,
        name="rope_tables",
    )(f_row)


def _attn_kernel(sink_ref, q_ref, kc_ref, kp_ref, vc_ref, vp_ref, cc_ref, sc_ref, cp_ref, sp_ref,
                 o_ref, *, pairs):
    n = pl.program_id(1)
    w = WINDOW
    half = V7X_LANES // 2
    cos_c, sin_c = cc_ref[...], sc_ref[...]

    def rope(x, cos, sin):
        return x * cos + pltpu.roll(x, half, 1) * sin

    lane = lax.broadcasted_iota(jnp.int32, (2 * w, V7X_LANES), 1)
    kcat = jnp.concatenate([rope(kp_ref[...], cp_ref[...], sp_ref[...]), rope(kc_ref[...], cos_c, sin_c)], axis=0)
    kv0_lanes = (lane & (half - 1)) < (half // 2)
    k_bd = jnp.concatenate([jnp.where(kv0_lanes, kcat, 0.0), jnp.where(kv0_lanes, 0.0, kcat)], axis=0).astype(BF16)
    vcat = jnp.concatenate([vp_ref[...], vc_ref[...]], axis=0)
    first = lane < half
    v_bd = jnp.concatenate([jnp.where(first, vcat, 0.0), jnp.where(first, 0.0, vcat)], axis=0)
    ones_bd = jnp.concatenate([jnp.where(first, 1.0, 0.0), jnp.where(first, 0.0, 1.0)], axis=0)
    v_aug = jnp.concatenate([v_bd, ones_bd], axis=1).astype(BF16)

    r = lax.broadcasted_iota(jnp.int32, (w, 2 * w), 0)
    j = lax.broadcasted_iota(jnp.int32, (w, 2 * w), 1)
    visible = (j > r) & (j <= r + w) & ((n > 0) | (j >= w))
    out_first = lax.broadcasted_iota(jnp.int32, (w, V7X_LANES), 1) < half

    scale = AT_HEAD_DIM ** -0.5
    cos_q, sin_q = cos_c * scale, sin_c * scale
    for p in range(pairs):
        ls = slice(p * V7X_LANES, (p + 1) * V7X_LANES)
        qr = rope(q_ref[:, ls], cos_q, sin_q).astype(BF16)
        s = _dot_nt(qr, k_bd)
        sink_a, sink_b = sink_ref[p], sink_ref[pairs + p]
        sa = jnp.where(visible, s[:, :2 * w], -jnp.inf)
        sb = jnp.where(visible, s[:, 2 * w:], -jnp.inf)
        ma = jnp.maximum(jnp.max(sa, axis=-1, keepdims=True), sink_a)
        mb = jnp.maximum(jnp.max(sb, axis=-1, keepdims=True), sink_b)
        pr = jnp.concatenate([jnp.exp(sa - ma), jnp.exp(sb - mb)], axis=1).astype(BF16)
        ov = _dot(pr, v_aug)
        den = ov[:, V7X_LANES:] + jnp.where(out_first, jnp.exp(sink_a - ma), jnp.exp(sink_b - mb))
        o_ref[:, ls] = (ov[:, :V7X_LANES] / den).astype(o_ref.dtype)


def _attention(proj, sinks, cos_t, sin_t, batch, seq, dn_width, at_width):
    w = WINDOW
    nb = seq // w
    pairs = at_width // V7X_LANES
    q_blk = (4 * dn_width) // at_width
    k_blk = (4 * dn_width + at_width) // V7X_LANES

    def cur(cb):
        return lambda b, n: (b * nb + n, cb)

    def prev(cb):
        return lambda b, n: (b * nb + jnp.maximum(n - 1, 0), cb)

    est = 2 * w * at_width * 4 + 2 * w * at_width * 2 + 16 * w * V7X_LANES * 4 + 64 * w * w * 4
    return pl.pallas_call(
        functools.partial(_attn_kernel, pairs=pairs),
        grid=(batch, nb),
        in_specs=[
            pl.BlockSpec(memory_space=pltpu.SMEM),
            pl.BlockSpec((w, at_width), cur(q_blk)),
            pl.BlockSpec((w, V7X_LANES), cur(k_blk)),
            pl.BlockSpec((w, V7X_LANES), prev(k_blk)),
            pl.BlockSpec((w, V7X_LANES), cur(k_blk + 1)),
            pl.BlockSpec((w, V7X_LANES), prev(k_blk + 1)),
            pl.BlockSpec((w, V7X_LANES), lambda b, n: (n, 0)),
            pl.BlockSpec((w, V7X_LANES), lambda b, n: (n, 0)),
            pl.BlockSpec((w, V7X_LANES), lambda b, n: (jnp.maximum(n - 1, 0), 0)),
            pl.BlockSpec((w, V7X_LANES), lambda b, n: (jnp.maximum(n - 1, 0), 0)),
        ],
        out_specs=pl.BlockSpec((w, at_width), lambda b, n: (b * nb + n, 0)),
        out_shape=jax.ShapeDtypeStruct((batch * seq, at_width), BF16),
        compiler_params=_cparams(("parallel", "arbitrary"), est),
        name="swa_sink_attention",
    )(sinks, proj, proj, proj, proj, proj, cos_t, sin_t, cos_t, sin_t)


def _out_proj_kernel(dn_ref, at_ref, w_ref, x_ref, mod_ref, o_ref, *, dn_width):
    acc = _dot(dn_ref[...], w_ref[:dn_width, :]) + _dot(at_ref[...], w_ref[dn_width:, :])
    o_ref[...] = x_ref[...] + mod_ref[2:3, :] * acc


def _out_proj(dn, at, w_bf16, x2, mod_l, tokens_per_batch, tm=512):
    n, d = x2.shape
    dnw, atw = dn.shape[1], at.shape[1]
    tm = min(tm, tokens_per_batch)
    tiles_per_batch = tokens_per_batch // tm
    est = 2 * (dnw + atw) * d * 2 + 2 * tm * (dnw + atw) * 2 + 4 * tm * d * 4 + tm * d * 4
    return pl.pallas_call(
        functools.partial(_out_proj_kernel, dn_width=dnw),
        grid=(n // tm,),
        in_specs=[
            pl.BlockSpec((tm, dnw), lambda i: (i, 0)),
            pl.BlockSpec((tm, atw), lambda i: (i, 0)),
            pl.BlockSpec((dnw + atw, d), lambda i: (0, 0)),
            pl.BlockSpec((tm, d), lambda i: (i, 0)),
            pl.BlockSpec((None, N_MOD, d), lambda i: (i // tiles_per_batch, 0, 0)),
        ],
        out_specs=pl.BlockSpec((tm, d), lambda i: (i, 0)),
        out_shape=jax.ShapeDtypeStruct((n, d), F32),
        compiler_params=_cparams(("parallel",), est),
        name="out_proj",
    )(dn, at, w_bf16, x2, mod_l)


def _ffn_kernel(x_ref, g_ref, mod_ref, wg_ref, wu_ref, wd_ref, o_ref, h_ref, acc_ref):
    f = pl.program_id(1)

    @pl.when(f == 0)
    def _():
        h = _norm_mod(x_ref[...], g_ref[...], mod_ref[3:4, :], mod_ref[4:5, :])
        h_ref[...] = h.astype(BF16)
        acc_ref[...] = jnp.zeros_like(acc_ref)

    h = h_ref[...]
    gate = _dot(h, wg_ref[...])
    up = _dot(h, wu_ref[...])
    acc_ref[...] += _dot((_silu(gate) * up).astype(BF16), wd_ref[...])

    @pl.when(f == pl.num_programs(1) - 1)
    def _():
        o_ref[...] = x_ref[...] + mod_ref[5:6, :] * acc_ref[...]


def _ffn(x2, gain, mod_l, w_gu_bf16, w_dn_bf16, tokens_per_batch, tm=512, tf=512):
    n, d = x2.shape
    ffn = w_dn_bf16.shape[0]
    tm = min(tm, tokens_per_batch)
    tiles_per_batch = tokens_per_batch // tm
    nf = ffn // tf
    est = 4 * tm * d * 4 + tm * d * 4 + tm * d * 2 + 6 * d * tf * 2 + 3 * tm * tf * 4 + 2 * tm * d * 4
    return pl.pallas_call(
        _ffn_kernel,
        grid=(n // tm, nf),
        in_specs=[
            pl.BlockSpec((tm, d), lambda i, f: (i, 0)),
            pl.BlockSpec((1, d), lambda i, f: (0, 0)),
            pl.BlockSpec((None, N_MOD, d), lambda i, f: (i // tiles_per_batch, 0, 0)),
            pl.BlockSpec((d, tf), lambda i, f: (0, f)),
            pl.BlockSpec((d, tf), lambda i, f: (0, nf + f)),
            pl.BlockSpec((tf, d), lambda i, f: (f, 0)),
        ],
        out_specs=pl.BlockSpec((tm, d), lambda i, f: (i, 0)),
        out_shape=jax.ShapeDtypeStruct((n, d), F32),
        scratch_shapes=[pltpu.VMEM((tm, d), BF16), pltpu.VMEM((tm, d), F32)],
        compiler_params=_cparams(("parallel", "arbitrary"), est),
        name="swiglu_ffn",
    )(x2, gain.reshape(1, d), mod_l, w_gu_bf16, w_gu_bf16, w_dn_bf16)


def _final_norm_kernel(x_ref, g_ref, o_ref):
    x = x_ref[...]
    var = jnp.mean(x * x, axis=-1, keepdims=True)
    o_ref[...] = x * lax.rsqrt(var + EPS) * g_ref[...]


def _final_norm(x2, gain, tm=512):
    n, d = x2.shape
    tm = min(tm, n)
    return pl.pallas_call(
        _final_norm_kernel,
        grid=(n // tm,),
        in_specs=[pl.BlockSpec((tm, d), lambda i: (i, 0)), pl.BlockSpec((1, d), lambda i: (0, 0))],
        out_specs=pl.BlockSpec((tm, d), lambda i: (i, 0)),
        out_shape=jax.ShapeDtypeStruct((n, d), F32),
        compiler_params=_cparams(("parallel",), 4 * tm * d * 4),
        name="final_norm",
    )(x2, gain.reshape(1, d))


def _paired_head_perm(n_heads):
    half = AT_HEAD_DIM // 2
    groups = n_heads // 2
    perm = np.empty(n_heads * AT_HEAD_DIM, np.int32)
    for p in range(groups):
        for part in range(2):
            for s in range(2):
                for i in range(half):
                    new = p * V7X_LANES + part * AT_HEAD_DIM + s * half + i
                    perm[new] = (p + groups * s) * AT_HEAD_DIM + part * half + i
    return perm


def _paired_out_perm(n_heads):
    groups = n_heads // 2
    perm = np.empty(n_heads * AT_HEAD_DIM, np.int32)
    for p in range(groups):
        for s in range(2):
            for dd in range(AT_HEAD_DIM):
                perm[p * V7X_LANES + s * AT_HEAD_DIM + dd] = (p + groups * s) * AT_HEAD_DIM + dd
    return perm


def kernel(x, c, ln_mix, ln_ffn, w_ada, b_ada, w_in, dn_conv_w, dn_a_log, dn_dt_bias, dn_norm_w,
           attn_sinks, w_out, w_gate_up, w_down, ln_final):
    batch, seq, d = x.shape
    depth = w_in.shape[0]
    dn_heads = dn_a_log.shape[1]
    dn_hd = dn_norm_w.shape[1]
    dn_width = dn_heads * dn_hd
    at_q_heads = attn_sinks.shape[1]
    at_width = at_q_heads * AT_HEAD_DIM
    kv_width = AT_KV_HEADS * AT_HEAD_DIM
    assert kv_width == V7X_LANES and at_width % V7X_LANES == 0 and seq % WINDOW == 0
    assert w_in.shape[2] == 4 * dn_width + 2 * dn_heads + at_width + 2 * kv_width
    assert 2 * dn_heads <= V7X_LANES

    o_z = 3 * dn_width
    o_b = o_z + dn_width
    o_a = o_b + dn_heads
    o_q = o_a + dn_heads
    o_k = o_q + at_width
    o_v = o_k + kv_width
    used = 4 * dn_width + at_width + 2 * kv_width + 2 * dn_heads
    in_tile = 512
    ncols = -(-(used - 2 * dn_heads + V7X_LANES) // in_tile) * in_tile
    cols = np.concatenate([
        np.arange(0, o_b),
        o_q + _paired_head_perm(at_q_heads),
        o_k + _paired_head_perm(AT_KV_HEADS),
        np.arange(o_v, o_v + kv_width),
        np.arange(o_b, o_q),
    ]).astype(np.int32)
    w_in_p = jnp.pad(jnp.take(w_in, cols, axis=2).astype(BF16), ((0, 0), (0, 0), (0, ncols - used)))
    rows_out = np.concatenate([np.arange(dn_width), dn_width + _paired_out_perm(at_q_heads)]).astype(np.int32)
    w_out_p = jnp.take(w_out, rows_out, axis=1).astype(BF16)
    w_gu = w_gate_up.astype(BF16)
    w_dn = w_down.astype(BF16)

    lanes_pad = V7X_LANES - 2 * dn_heads
    alog_rows = jnp.pad(dn_a_log, ((0, 0), (dn_heads, lanes_pad)))
    dtb_rows = jnp.pad(dn_dt_bias, ((0, 0), (dn_heads, lanes_pad)))

    c_pad = jnp.pad(c, ((0, V7X_SUBLANES - batch % V7X_SUBLANES), (0, 0))) if batch % V7X_SUBLANES else c
    mod = _adaln(c_pad, w_ada, b_ada)[:, :batch].reshape(depth, batch, N_MOD, d)
    cos_t, sin_t = _rope_tables(seq)

    x2 = x.reshape(batch * seq, d)
    for l in range(depth):
        proj = _in_proj(x2, ln_mix[l], mod[l], w_in_p[l], seq)
        dn = _deltanet(proj, dn_conv_w[l], alog_rows[l:l + 1], dtb_rows[l:l + 1], dn_norm_w[l],
                       batch, seq, dn_heads, dn_hd, (used - 2 * dn_heads) // V7X_LANES)
        at = _attention(proj, attn_sinks[l], cos_t, sin_t, batch, seq, dn_width, at_width)
        x2 = _out_proj(dn, at, w_out_p[l], x2, mod[l], seq)
        x2 = _ffn(x2, ln_ffn[l], mod[l], w_gu[l], w_dn[l], seq)
    return _final_norm(x2, ln_final).reshape(batch, seq, d)
```

```python
import functools

import jax
import jax.numpy as jnp
from jax import lax
from jax.experimental import pallas as pl
from jax.experimental.pallas import tpu as pltpu

F32 = jnp.float32
BF16 = jnp.bfloat16

AT_HEAD_DIM = 64
AT_KV_HEADS = 2
CHUNK = 64
WINDOW = 128
ROPE_THETA = 10000.0
EPS = 1e-6
N_MOD = 6

V7X_LANES = 128
V7X_SUBLANES = 8
V7X_VMEM_BYTES = 64 * 1024 * 1024
VMEM_REQUEST_CAP = 60000 * 1024


def _cparams(semantics, vmem_estimate_bytes):
    limit = min(max(int(vmem_estimate_bytes * 1.25), 16 * 1024 * 1024), VMEM_REQUEST_CAP)
    return pltpu.CompilerParams(dimension_semantics=semantics, vmem_limit_bytes=limit)


def _dot(a, b):
    return jnp.dot(a, b, preferred_element_type=F32)


def _dot_nt(a, b):
    return lax.dot_general(a, b, (((1,), (1,)), ((), ())), preferred_element_type=F32)


def _dot_tn(a, b):
    return lax.dot_general(a, b, (((0,), (0,)), ((), ())), preferred_element_type=F32)


def _split3(x):
    x1 = x.astype(BF16)
    r1 = x - x1.astype(F32)
    x2 = r1.astype(BF16)
    x3 = (r1 - x2.astype(F32)).astype(BF16)
    return x1, x2, x3


def _dot_sel(sel_bf16, x):
    x1, x2, x3 = _split3(x)
    return _dot(sel_bf16, x1) + _dot(sel_bf16, x2) + _dot(sel_bf16, x3)


def _div_pow2(x, n):
    shift = n.bit_length() - 1
    assert n == 1 << shift
    return lax.shift_right_arithmetic(x, shift)


def _silu(x):
    return x * jax.nn.sigmoid(x)


def _softplus(x):
    return jnp.maximum(x, 0.0) + jnp.log1p(jnp.exp(-jnp.abs(x)))


NORM_ROWS = 128


def _norm_mod_rows(x_ref, h_ref, gain, shift, scale):
    geff = gain * (1.0 + scale)

    def body(i, carry):
        r = pl.ds(pl.multiple_of(i * NORM_ROWS, NORM_ROWS), NORM_ROWS)
        x = x_ref[r, :]
        inv = lax.rsqrt(jnp.mean(x * x, axis=-1, keepdims=True) + EPS)
        h_ref[r, :] = (x * inv * geff + shift).astype(h_ref.dtype)
        return carry

    lax.fori_loop(0, x_ref.shape[0] // NORM_ROWS, body, 0)


def _adaln_kernel(c_ref, w_ref, b_ref, o_ref):
    ca = _silu(c_ref[...]).astype(BF16)
    o_ref[...] = _dot(ca, w_ref[...].astype(BF16)) + b_ref[...]


def _adaln(c_pad, w_ada, b_ada, tn=1024):
    depth, d, n = w_ada.shape
    rows = c_pad.shape[0]
    est = 2 * d * tn * 4 + d * tn * 2 + 4 * rows * (d + tn) * 4
    return pl.pallas_call(
        _adaln_kernel,
        grid=(depth, n // tn),
        in_specs=[
            pl.BlockSpec((rows, d), lambda l, j: (0, 0)),
            pl.BlockSpec((None, d, tn), lambda l, j: (l, 0, j)),
            pl.BlockSpec((None, 1, tn), lambda l, j: (l, 0, j)),
        ],
        out_specs=pl.BlockSpec((None, rows, tn), lambda l, j: (l, 0, j)),
        out_shape=jax.ShapeDtypeStruct((depth, rows, n), F32),
        compiler_params=_cparams(("arbitrary", "arbitrary"), est),
        name="adaln_mod",
    )(c_pad, w_ada, b_ada.reshape(depth, 1, n))


def _in_proj_kernel(x_ref, g_ref, mod_ref, w_ref, o_ref, h_ref):
    @pl.when(pl.program_id(1) == 0)
    def _():
        _norm_mod_rows(x_ref, h_ref, g_ref[...], mod_ref[0:1, :], mod_ref[1:2, :])

    o_ref[...] = _dot(h_ref[...], w_ref[...])


def _in_proj(x2, gains, mod, w_bf16, layer, tokens_per_batch, tm=1024, tn=512):
    n, d = x2.shape
    ncols = w_bf16.shape[2]
    tm = min(tm, tokens_per_batch)
    tiles_per_batch = tokens_per_batch // tm
    est = 2 * tm * d * 4 + tm * d * 2 + 2 * d * tn * 2 + 2 * tm * tn * 4 + tm * tn * 4
    return pl.pallas_call(
        _in_proj_kernel,
        grid=(n // tm, ncols // tn),
        in_specs=[
            pl.BlockSpec((tm, d), lambda i, j: (i, 0)),
            pl.BlockSpec((None, 1, d), lambda i, j: (layer, 0, 0)),
            pl.BlockSpec((None, None, N_MOD, d), lambda i, j: (layer, i // tiles_per_batch, 0, 0)),
            pl.BlockSpec((None, d, tn), lambda i, j: (layer, 0, j)),
        ],
        out_specs=pl.BlockSpec((tm, tn), lambda i, j: (i, j)),
        out_shape=jax.ShapeDtypeStruct((n, ncols), F32),
        scratch_shapes=[pltpu.VMEM((tm, d), BF16)],
        compiler_params=_cparams(("parallel", "arbitrary"), est),
        name="in_proj",
    )(x2, gains, mod, w_bf16)


GROUP = 4


def _block_diag(x, mask):
    return jnp.where(mask, jnp.concatenate([x] * GROUP, axis=0), 0.0).astype(BF16)


def _interleave(gens):
    gens = list(gens)
    while gens:
        for gen in list(gens):
            try:
                next(gen)
            except StopIteration:
                gens.remove(gen)


def _dn_kernel(q_ref, k_ref, v_ref, z_ref, ba_ref, cw_ref, alog_ref, dtb_ref, nw_ref, o_ref,
               ext_ref, tail_ref, qs_ref, ks_ref, vs_ref, s_ref, *, tb, heads, hd, conv_k):
    width = heads * hd
    t = pl.program_id(1)

    @pl.when(t == 0)
    def _():
        tail_ref[...] = jnp.zeros_like(tail_ref)
        s_ref[...] = jnp.zeros_like(s_ref)

    for seg, (src, dst) in enumerate(((q_ref, qs_ref), (k_ref, ks_ref), (v_ref, vs_ref))):
        cur = src[...]
        ext_ref[0:V7X_SUBLANES, :] = tail_ref[seg]
        ext_ref[V7X_SUBLANES:, :] = cur
        lo, hi = seg * width, (seg + 1) * width
        acc = cur * cw_ref[conv_k - 1:conv_k, lo:hi]
        for j in range(1, conv_k):
            acc = acc + ext_ref[pl.ds(V7X_SUBLANES - j, tb), :] * cw_ref[conv_k - 1 - j:conv_k - j, lo:hi]
        tail_ref[seg] = cur[tb - V7X_SUBLANES:, :]
        y = _silu(acc)
        if seg == 2:
            dst[...] = y
        else:
            post = hd ** -0.5 if seg == 0 else 1.0
            for h in range(heads):
                yh = y[:, h * hd:(h + 1) * hd]
                inv = lax.rsqrt(jnp.sum(yh * yh, axis=-1, keepdims=True) + EPS)
                dst[:, h * hd:(h + 1) * hd] = yh * (inv * post) if post != 1.0 else yh * inv

    ba = ba_ref[...]
    lane = lax.broadcasted_iota(jnp.int32, (tb, V7X_LANES), 1)
    beta_all = jax.nn.sigmoid(ba)
    g_all = jnp.where((lane >= heads) & (lane < 2 * heads),
                      -jnp.exp(alog_ref[...]) * _softplus(ba + dtb_ref[...]), 0.0)

    ri = lax.broadcasted_iota(jnp.int32, (tb, tb), 0)
    ci = lax.broadcasted_iota(jnp.int32, (tb, tb), 1)
    same_chunk = _div_pow2(ri, CHUNK) == _div_pow2(ci, CHUNK)
    cum_sel = jnp.where(same_chunk & (ci <= ri), 1.0, 0.0).astype(BF16)
    tot_sel = jnp.where(same_chunk, 1.0, 0.0).astype(BF16)
    gc_all = _dot_sel(cum_sel, g_all)
    gtot_all = _dot_sel(tot_sel, g_all)

    half = V7X_LANES // 2
    step = 2 * heads
    g1, g2, g3 = _split3(gc_all)
    wp = g1.astype(F32) + pltpu.roll(g2.astype(F32), step, 1) + pltpu.roll(g3.astype(F32), 2 * step, 1)
    wq = pltpu.roll(wp, half, 1)
    lu = lane - half - heads
    upper_any = (lu >= 0) & (lu < 3 * step) & ((lu & (step - 1)) < heads)
    u_all = (wp + jnp.where(upper_any, 1.0, 0.0)).astype(BF16)

    n_groups = heads // GROUP
    rows_g = GROUP * CHUNK
    lane_g = lax.broadcasted_iota(jnp.int32, (rows_g, V7X_LANES), 1)
    head_g = _div_pow2(lax.broadcasted_iota(jnp.int32, (rows_g, V7X_LANES), 0), CHUNK)
    pick_lo, pick_hi = [], []
    for g in range(n_groups):
        off = lane_g - heads - g * GROUP - head_g
        pick_lo.append((off == 0) | (off == step) | (off == 2 * step))
        offh = off - half
        pick_hi.append((offh == 0) | (offh == step) | (offh == 2 * step))

    ii = lax.broadcasted_iota(jnp.int32, (CHUNK, GROUP * CHUNK), 0)
    jj = lax.broadcasted_iota(jnp.int32, (CHUNK, GROUP * CHUNK), 1) & (CHUNK - 1)
    lower = ii >= jj
    strict = ii > jj
    eye = jnp.where(ii == jj, 1.0, 0.0).astype(F32)
    i16, j16, i32, j32 = _div_pow2(ii, 16), _div_pow2(jj, 16), _div_pow2(ii, 32), _div_pow2(jj, 32)
    m_diag16 = i16 == j16
    m_off32 = (i32 == j32) & (i16 > j16)
    m_off64 = i32 > j32
    bd_rows = _div_pow2(lax.broadcasted_iota(jnp.int32, (rows_g, GROUP * hd), 0), CHUNK)
    bd_w = (_div_pow2(lax.broadcasted_iota(jnp.int32, (rows_g, rows_g), 0), CHUNK)
            == _div_pow2(lax.broadcasted_iota(jnp.int32, (rows_g, rows_g), 1), CHUNK))
    bd_k = bd_rows == _div_pow2(lax.broadcasted_iota(jnp.int32, (rows_g, GROUP * hd), 1), hd)
    nw = nw_ref[...]
    stash = {}

    def parallel_part(c):
        r = slice(c * CHUNK, (c + 1) * CHUNK)
        qg, kd = [None] * heads, [None] * heads
        a4, rhs, intra = [], [], []
        for g in range(n_groups):
            kbs, qs_, ks_, wrhs = [], [], [], []
            for hh in range(GROUP):
                h = g * GROUP + hh
                hs = slice(h * hd, (h + 1) * hd)
                gcol = gc_all[r, heads + h:heads + h + 1]
                gtot = gtot_all[r, heads + h:heads + h + 1]
                beta = beta_all[r, h:h + 1]
                egc = jnp.exp(gcol)
                q, k, v = qs_ref[r, hs], ks_ref[r, hs], vs_ref[r, hs]
                kb = k * beta
                kbs.append(kb)
                qs_.append(q)
                ks_.append(k)
                wrhs.append(jnp.concatenate([kb * egc, v * beta], axis=1))
                qg[h] = (q * egc).astype(BF16)
                kd[h] = (k * jnp.exp(gtot - gcol)).astype(BF16)
            lhs = jnp.concatenate([jnp.concatenate(kbs, axis=1), jnp.concatenate(qs_, axis=1)], axis=0)
            kq = _dot_nt(lhs.astype(BF16), _block_diag(jnp.concatenate(ks_, axis=1), bd_k))
            wq4 = jnp.concatenate([wq[r]] * GROUP, axis=0)
            vg = jnp.where(pick_hi[g], -wq4, jnp.where(pick_lo[g], 1.0, 0.0)).astype(BF16)
            decay = jnp.exp(jnp.where(lower, _dot_nt(u_all[r], vg), -jnp.inf))
            a4.append(jnp.where(strict, kq[:CHUNK] * decay, 0.0))
            intra.append((kq[CHUNK:] * decay).astype(BF16))
            rhs.append(jnp.concatenate(wrhs, axis=0).astype(BF16))
        yield
        b = [jnp.where(m_diag16, a, 0.0) for a in a4]
        p = [eye - x for x in b]
        b = [_dot(x.astype(BF16), _block_diag(x, bd_w)) for x in b]
        yield
        for _ in range(2):
            pb = [_dot(jnp.concatenate([x, y], axis=0).astype(BF16), _block_diag(y, bd_w)) for x, y in zip(p, b)]
            p = [x + y[:CHUNK] for x, y in zip(p, pb)]
            b = [y[CHUNK:] for y in pb]
            yield
        p = [x + _dot(x.astype(BF16), _block_diag(y, bd_w)) for x, y in zip(p, b)]
        yield
        for m in (m_off32, m_off64):
            tt = [_dot(jnp.where(m, a, 0.0).astype(BF16), _block_diag(x, bd_w)) for a, x in zip(a4, p)]
            yield
            p = [x - _dot(x.astype(BF16), _block_diag(y, bd_w)) for x, y in zip(p, tt)]
            yield
        wu = [_dot(_block_diag(x, bd_w), y) for x, y in zip(p, rhs)]
        w = [wu[h // GROUP][(h % GROUP) * CHUNK:(h % GROUP + 1) * CHUNK, :hd].astype(BF16) for h in range(heads)]
        u = [wu[h // GROUP][(h % GROUP) * CHUNK:(h % GROUP + 1) * CHUNK, hd:] for h in range(heads)]
        r8 = slice(c * CHUNK, c * CHUNK + V7X_SUBLANES)
        gl = jnp.concatenate([jnp.broadcast_to(jnp.exp(gtot_all[r8, heads + h:heads + h + 1]), (V7X_SUBLANES, hd))
                              for h in range(heads)], axis=1)
        stash[c] = (w, u, qg, kd, intra, gl)

    def recurrent_part(c):
        r = slice(c * CHUNK, (c + 1) * CHUNK)
        w, u, qg, kd, intra, gl = stash.pop(c)
        zs = jnp.zeros((hd, hd), BF16)
        zc = jnp.zeros((CHUNK, hd), BF16)
        states, ws = [], []
        for pr in range(heads // 2):
            h0, h1 = 2 * pr, 2 * pr + 1
            s2 = s_ref[pr]
            sb = s2.astype(BF16)
            bds = jnp.concatenate([jnp.concatenate([sb[:, :hd], zs], axis=1),
                                   jnp.concatenate([zs, sb[:, hd:]], axis=1)], axis=0)
            lhs = jnp.concatenate([jnp.concatenate([w[h0], w[h1]], axis=1),
                                   jnp.concatenate([qg[h0], qg[h1]], axis=1)], axis=0)
            states.append(s2)
            ws.append(_dot(lhs, bds))
        yield
        outs = []
        for pr in range(heads // 2):
            h0, h1 = 2 * pr, 2 * pr + 1
            vnb = (jnp.concatenate([u[h0], u[h1]], axis=1) - ws[pr][:CHUNK]).astype(BF16)
            bdv = jnp.concatenate([jnp.concatenate([vnb[:, :hd], zc], axis=1),
                                   jnp.concatenate([zc, vnb[:, hd:]], axis=1)], axis=0)
            g, sub = divmod(pr, GROUP // 2)
            intra2 = intra[g][:, sub * 2 * CHUNK:(sub + 1) * 2 * CHUNK]
            outs.append(ws[pr][CHUNK:] + _dot(intra2, bdv))
            ds = _dot_tn(jnp.concatenate([kd[h0], kd[h1]], axis=0), bdv)
            s_ref[pr] = states[pr] * gl[0:1, pr * 2 * hd:(pr + 1) * 2 * hd] + ds
        yield
        for pr in range(heads // 2):
            for sub in range(2):
                h = 2 * pr + sub
                hs = slice(h * hd, (h + 1) * hd)
                o = outs[pr][:, sub * hd:(sub + 1) * hd]
                on = o * lax.rsqrt(jnp.mean(o * o, axis=-1, keepdims=True) + EPS) * nw
                o_ref[r, hs] = (on * _silu(z_ref[r, hs])).astype(o_ref.dtype)

    n_chunks = tb // CHUNK
    _interleave([parallel_part(0)])
    for c in range(n_chunks):
        nxt = [parallel_part(c + 1)] if c + 1 < n_chunks else []
        _interleave(nxt + [recurrent_part(c)])


def _deltanet(proj, conv_w, alog_rows, dtb_rows, norm_w, layer, batch, seq, heads, hd, ba_blk, tb=256):
    width = heads * hd
    conv_k = conv_w.shape[1]
    tb = min(tb, seq)
    assert conv_k - 1 <= V7X_SUBLANES and tb % CHUNK == 0 and seq % tb == 0
    assert heads % GROUP == 0 and 6 * heads + heads <= V7X_LANES // 2
    nt = seq // tb

    def col(cb):
        return lambda b, t: (b * nt + t, cb)

    est = (2 * 4 * tb * width * 4 + 2 * tb * width * 2 + (tb + V7X_SUBLANES) * width * 4
           + 3 * V7X_SUBLANES * width * 4 + 3 * tb * width * 4 + heads * hd * hd * 4
           + 2 * conv_k * 3 * width * 4 + 8 * tb * tb * 4)
    kern = functools.partial(_dn_kernel, tb=tb, heads=heads, hd=hd, conv_k=conv_k)
    return pl.pallas_call(
        kern,
        grid=(batch, nt),
        in_specs=[
            pl.BlockSpec((tb, width), col(0)),
            pl.BlockSpec((tb, width), col(1)),
            pl.BlockSpec((tb, width), col(2)),
            pl.BlockSpec((tb, width), col(3)),
            pl.BlockSpec((tb, V7X_LANES), col(ba_blk)),
            pl.BlockSpec((None, conv_k, 3 * width), lambda b, t: (layer, 0, 0)),
            pl.BlockSpec((None, 1, V7X_LANES), lambda b, t: (layer, 0, 0)),
            pl.BlockSpec((None, 1, V7X_LANES), lambda b, t: (layer, 0, 0)),
            pl.BlockSpec((None, 1, hd), lambda b, t: (layer, 0, 0)),
        ],
        out_specs=pl.BlockSpec((tb, width), lambda b, t: (b * nt + t, 0)),
        out_shape=jax.ShapeDtypeStruct((batch * seq, width), BF16),
        scratch_shapes=[
            pltpu.VMEM((tb + V7X_SUBLANES, width), F32),
            pltpu.VMEM((3, V7X_SUBLANES, width), F32),
            pltpu.VMEM((tb, width), F32),
            pltpu.VMEM((tb, width), F32),
            pltpu.VMEM((tb, width), F32),
            pltpu.VMEM((heads // 2, hd, 2 * hd), F32),
        ],
        compiler_params=_cparams(("parallel", "arbitrary"), est),
        name="gated_deltanet",
    )(proj, proj, proj, proj, proj, conv_w, alog_rows, dtb_rows, norm_w)


def _rope_kernel(f_ref, cos_ref, sin_ref, *, tr):
    pos = lax.broadcasted_iota(jnp.int32, (tr, V7X_LANES), 0) + pl.program_id(0) * tr
    ang = pos.astype(F32) * f_ref[...]
    lane = lax.broadcasted_iota(jnp.int32, (tr, V7X_LANES), 1)
    cos_ref[...] = jnp.cos(ang)
    sin_ref[...] = jnp.where(lane < V7X_LANES // 2, -jnp.sin(ang), jnp.sin(ang))


def _rope_tables(seq, tr=512):
    half = AT_HEAD_DIM // 2
    inv_freq = ROPE_THETA ** (-jnp.arange(half, dtype=F32) * 2.0 / AT_HEAD_DIM)
    f_row = jnp.tile(inv_freq, V7X_LANES // half).reshape(1, V7X_LANES)
    tr = min(tr, seq)
    return pl.pallas_call(
        functools.partial(_rope_kernel, tr=tr),
        grid=(seq // tr,),
        in_specs=[pl.BlockSpec((1, V7X_LANES), lambda i: (0, 0))],
        out_specs=[pl.BlockSpec((tr, V7X_LANES), lambda i: (i, 0))] * 2,
        out_shape=[jax.ShapeDtypeStruct((seq, V7X_LANES), F32)] * 2,
        name="rope_tables",
    )(f_row)


def _attn_kernel(sink_ref, q_ref, kc_ref, kp_ref, vc_ref, vp_ref, cc_ref, sc_ref, cp_ref, sp_ref,
                 o_ref, *, pairs, layer, tq):
    n = pl.program_id(1)
    w = WINDOW
    half = V7X_LANES // 2
    cos_c, sin_c = cc_ref[...], sc_ref[...]

    def rope(x, cos, sin):
        return x * cos + pltpu.roll(x, half, 1) * sin

    lane = lax.broadcasted_iota(jnp.int32, (w + tq, V7X_LANES), 1)
    kcat = jnp.concatenate([rope(kp_ref[...], cp_ref[...], sp_ref[...]), rope(kc_ref[...], cos_c, sin_c)], axis=0)
    kv0_lanes = (lane & (half - 1)) < (half // 2)
    k0 = jnp.where(kv0_lanes, kcat, 0.0).astype(BF16)
    k1 = jnp.where(kv0_lanes, 0.0, kcat).astype(BF16)
    vcat = jnp.concatenate([vp_ref[...], vc_ref[...]], axis=0)
    first = lane < half
    v0 = jnp.concatenate([jnp.where(first, vcat, 0.0), jnp.where(first, 1.0, 0.0)], axis=1).astype(BF16)
    v1 = jnp.concatenate([jnp.where(first, 0.0, vcat), jnp.where(first, 0.0, 1.0)], axis=1).astype(BF16)

    r = lax.broadcasted_iota(jnp.int32, (w, 2 * w), 0)
    j = lax.broadcasted_iota(jnp.int32, (w, 2 * w), 1)
    in_band = (j > r) & (j <= r + w)
    out_first = lax.broadcasted_iota(jnp.int32, (w, V7X_LANES), 1) < half
    scale = AT_HEAD_DIM ** -0.5
    cos_q, sin_q = cos_c * scale, sin_c * scale

    visible_any = jnp.concatenate([in_band] * pairs, axis=0)
    visible_first = jnp.concatenate([in_band & ((n > 0) | (j >= w))] * pairs, axis=0)
    sink_a = jnp.concatenate([jnp.full((w, V7X_LANES), sink_ref[layer, p], F32) for p in range(pairs)], axis=0)
    sink_b = jnp.concatenate([jnp.full((w, V7X_LANES), sink_ref[layer, pairs + p], F32) for p in range(pairs)], axis=0)
    out_first_all = jnp.concatenate([out_first] * pairs, axis=0)

    def row_max(x, sink):
        return jnp.maximum(jnp.broadcast_to(jnp.max(x, axis=-1, keepdims=True), sink.shape), sink)

    def scores(sub):
        rows = slice(sub * w, (sub + 1) * w)
        keys = slice(sub * w, (sub + 2) * w)
        qs = [rope(q_ref[rows, p * V7X_LANES:(p + 1) * V7X_LANES], cos_q[rows], sin_q[rows]).astype(BF16)
              for p in range(pairs)]
        return _dot_nt(jnp.concatenate(qs, axis=0), jnp.concatenate([k0[keys], k1[keys]], axis=0))

    n_sub = tq // w
    s_next = scores(0)
    for sub in range(n_sub):
        s = s_next
        if sub + 1 < n_sub:
            s_next = scores(sub + 1)
        rows = slice(sub * w, (sub + 1) * w)
        keys = slice(sub * w, (sub + 2) * w)
        visible = visible_first if sub == 0 else visible_any
        sa = jnp.where(visible, s[:, :2 * w], -jnp.inf)
        sb = jnp.where(visible, s[:, 2 * w:], -jnp.inf)
        ma = row_max(sa, sink_a)
        mb = row_max(sb, sink_b)
        pr = jnp.concatenate([jnp.exp(sa - jnp.concatenate([ma, ma], axis=1)),
                              jnp.exp(sb - jnp.concatenate([mb, mb], axis=1))], axis=1).astype(BF16)
        ov = _dot(pr, jnp.concatenate([v0[keys], v1[keys]], axis=0))
        den = ov[:, V7X_LANES:] + jnp.where(out_first_all, jnp.exp(sink_a - ma), jnp.exp(sink_b - mb))
        out = (ov[:, :V7X_LANES] / den).astype(o_ref.dtype)
        for p in range(pairs):
            o_ref[rows, p * V7X_LANES:(p + 1) * V7X_LANES] = out[p * w:(p + 1) * w]


def _attention(proj, sinks, cos_t, sin_t, layer, batch, seq, dn_width, at_width, tq=256):
    w = WINDOW
    tq = min(tq, seq)
    assert tq % w == 0 and seq % tq == 0
    nb = seq // tq
    wpb = tq // w
    pairs = at_width // V7X_LANES
    q_blk = (4 * dn_width) // at_width
    k_blk = (4 * dn_width + at_width) // V7X_LANES

    def cur(cb):
        return lambda b, n: (b * nb + n, cb)

    def prev(cb):
        return lambda b, n: (b * nb * wpb + jnp.maximum(n * wpb - 1, 0), cb)

    def prev_t(b, n):
        return (jnp.maximum(n * wpb - 1, 0), 0)

    est = 2 * tq * at_width * 4 + 2 * tq * at_width * 2 + 24 * (tq + w) * V7X_LANES * 4 + 64 * w * w * 4
    return pl.pallas_call(
        functools.partial(_attn_kernel, pairs=pairs, layer=layer, tq=tq),
        grid=(batch, nb),
        in_specs=[
            pl.BlockSpec(memory_space=pltpu.SMEM),
            pl.BlockSpec((tq, at_width), cur(q_blk)),
            pl.BlockSpec((tq, V7X_LANES), cur(k_blk)),
            pl.BlockSpec((w, V7X_LANES), prev(k_blk)),
            pl.BlockSpec((tq, V7X_LANES), cur(k_blk + 1)),
            pl.BlockSpec((w, V7X_LANES), prev(k_blk + 1)),
            pl.BlockSpec((tq, V7X_LANES), lambda b, n: (n, 0)),
            pl.BlockSpec((tq, V7X_LANES), lambda b, n: (n, 0)),
            pl.BlockSpec((w, V7X_LANES), prev_t),
            pl.BlockSpec((w, V7X_LANES), prev_t),
        ],
        out_specs=pl.BlockSpec((tq, at_width), lambda b, n: (b * nb + n, 0)),
        out_shape=jax.ShapeDtypeStruct((batch * seq, at_width), BF16),
        compiler_params=_cparams(("parallel", "arbitrary"), est),
        name="swa_sink_attention",
    )(sinks, proj, proj, proj, proj, proj, cos_t, sin_t, cos_t, sin_t)


def _out_proj_kernel(dn_ref, at_ref, w_ref, x_ref, mod_ref, o_ref, *, dn_width):
    acc = _dot(dn_ref[...], w_ref[:dn_width, :]) + _dot(at_ref[...], w_ref[dn_width:, :])
    o_ref[...] = x_ref[...] + mod_ref[2:3, :] * acc


def _out_proj(dn, at, w_bf16, x2, mod, layer, tokens_per_batch, tm=512):
    n, d = x2.shape
    dnw, atw = dn.shape[1], at.shape[1]
    tm = min(tm, tokens_per_batch)
    tiles_per_batch = tokens_per_batch // tm
    est = 2 * (dnw + atw) * d * 2 + 2 * tm * (dnw + atw) * 2 + 4 * tm * d * 4 + tm * d * 4
    return pl.pallas_call(
        functools.partial(_out_proj_kernel, dn_width=dnw),
        grid=(n // tm,),
        in_specs=[
            pl.BlockSpec((tm, dnw), lambda i: (i, 0)),
            pl.BlockSpec((tm, atw), lambda i: (i, 0)),
            pl.BlockSpec((None, dnw + atw, d), lambda i: (layer, 0, 0)),
            pl.BlockSpec((tm, d), lambda i: (i, 0)),
            pl.BlockSpec((None, None, N_MOD, d), lambda i: (layer, i // tiles_per_batch, 0, 0)),
        ],
        out_specs=pl.BlockSpec((tm, d), lambda i: (i, 0)),
        out_shape=jax.ShapeDtypeStruct((n, d), F32),
        compiler_params=_cparams(("parallel",), est),
        name="out_proj",
    )(dn, at, w_bf16, x2, mod)


def _ffn_kernel(x_ref, g_ref, mod_ref, wg_ref, wu_ref, wd_ref, fg_ref, o_ref, h_ref, acc_ref, *, final_norm):
    f = pl.program_id(1)

    @pl.when(f == 0)
    def _():
        _norm_mod_rows(x_ref, h_ref, g_ref[...], mod_ref[3:4, :], mod_ref[4:5, :])
        acc_ref[...] = jnp.zeros_like(acc_ref)

    h = h_ref[...]
    gate = _dot(h, wg_ref[...])
    up = _dot(h, wu_ref[...])
    acc_ref[...] += _dot((_silu(gate) * up).astype(BF16), wd_ref[...])

    @pl.when(f == pl.num_programs(1) - 1)
    def _():
        gate_f = mod_ref[5:6, :]
        final_gain = fg_ref[...]

        def body(i, carry):
            r = pl.ds(pl.multiple_of(i * NORM_ROWS, NORM_ROWS), NORM_ROWS)
            y = x_ref[r, :] + gate_f * acc_ref[r, :]
            if final_norm:
                y = y * lax.rsqrt(jnp.mean(y * y, axis=-1, keepdims=True) + EPS) * final_gain
            o_ref[r, :] = y
            return carry

        lax.fori_loop(0, x_ref.shape[0] // NORM_ROWS, body, 0)


def _ffn(x2, gains, mod, w_gu_bf16, w_dn_bf16, final_gain, layer, tokens_per_batch, final_norm, tm=512, tf=512):
    n, d = x2.shape
    ffn = w_dn_bf16.shape[1]
    tm = min(tm, tokens_per_batch)
    tiles_per_batch = tokens_per_batch // tm
    nf = ffn // tf
    est = 4 * tm * d * 4 + tm * d * 4 + tm * d * 2 + 6 * d * tf * 2 + 4 * tm * tf * 4
    return pl.pallas_call(
        functools.partial(_ffn_kernel, final_norm=final_norm),
        grid=(n // tm, nf),
        in_specs=[
            pl.BlockSpec((tm, d), lambda i, f: (i, 0)),
            pl.BlockSpec((None, 1, d), lambda i, f: (layer, 0, 0)),
            pl.BlockSpec((None, None, N_MOD, d), lambda i, f: (layer, i // tiles_per_batch, 0, 0)),
            pl.BlockSpec((None, d, tf), lambda i, f: (layer, 0, f)),
            pl.BlockSpec((None, d, tf), lambda i, f: (layer, 0, nf + f)),
            pl.BlockSpec((None, tf, d), lambda i, f: (layer, f, 0)),
            pl.BlockSpec((1, d), lambda i, f: (0, 0)),
        ],
        out_specs=pl.BlockSpec((tm, d), lambda i, f: (i, 0)),
        out_shape=jax.ShapeDtypeStruct((n, d), F32),
        scratch_shapes=[pltpu.VMEM((tm, d), BF16), pltpu.VMEM((tm, d), F32)],
        compiler_params=_cparams(("parallel", "arbitrary"), est),
        name="swiglu_ffn",
    )(x2, gains, mod, w_gu_bf16, w_gu_bf16, w_dn_bf16, final_gain)


def _pair_head_cols(w, n_heads):
    half = AT_HEAD_DIM // 2
    groups = n_heads // 2
    lead = w.shape[:-1]
    w = w.reshape(*lead, 2, groups, 2, half)
    return jnp.moveaxis(w, -4, -2).reshape(*lead, n_heads * AT_HEAD_DIM)


def _pair_head_rows(w, n_heads):
    groups = n_heads // 2
    lead, d = w.shape[:-2], w.shape[-1]
    w = w.reshape(*lead, 2, groups, AT_HEAD_DIM, d)
    return jnp.swapaxes(w, -4, -3).reshape(*lead, n_heads * AT_HEAD_DIM, d)


def kernel(x, c, ln_mix, ln_ffn, w_ada, b_ada, w_in, dn_conv_w, dn_a_log, dn_dt_bias, dn_norm_w,
           attn_sinks, w_out, w_gate_up, w_down, ln_final):
    batch, seq, d = x.shape
    depth = w_in.shape[0]
    dn_heads = dn_a_log.shape[1]
    dn_hd = dn_norm_w.shape[1]
    dn_width = dn_heads * dn_hd
    at_q_heads = attn_sinks.shape[1]
    at_width = at_q_heads * AT_HEAD_DIM
    kv_width = AT_KV_HEADS * AT_HEAD_DIM
    assert kv_width == V7X_LANES and at_width % V7X_LANES == 0 and seq % WINDOW == 0
    assert w_in.shape[2] == 4 * dn_width + 2 * dn_heads + at_width + 2 * kv_width
    assert 2 * dn_heads <= V7X_LANES

    o_z = 3 * dn_width
    o_b = o_z + dn_width
    o_a = o_b + dn_heads
    o_q = o_a + dn_heads
    o_k = o_q + at_width
    o_v = o_k + kv_width
    used = 4 * dn_width + at_width + 2 * kv_width + 2 * dn_heads
    in_tile = 512
    ncols = -(-(used - 2 * dn_heads + V7X_LANES) // in_tile) * in_tile
    w_in_p = jnp.concatenate([
        w_in[..., :o_b].astype(BF16),
        _pair_head_cols(w_in[..., o_q:o_k], at_q_heads).astype(BF16),
        _pair_head_cols(w_in[..., o_k:o_v], AT_KV_HEADS).astype(BF16),
        w_in[..., o_v:].astype(BF16),
        w_in[..., o_b:o_q].astype(BF16),
        jnp.zeros((depth, d, ncols - used), BF16),
    ], axis=-1)
    w_out_p = jnp.concatenate([
        w_out[:, :dn_width].astype(BF16),
        _pair_head_rows(w_out[:, dn_width:], at_q_heads).astype(BF16),
    ], axis=1)
    w_gu = w_gate_up.astype(BF16)
    w_dn = w_down.astype(BF16)

    lanes_pad = V7X_LANES - 2 * dn_heads
    alog_rows = jnp.pad(dn_a_log, ((0, 0), (dn_heads, lanes_pad))).reshape(depth, 1, V7X_LANES)
    dtb_rows = jnp.pad(dn_dt_bias, ((0, 0), (dn_heads, lanes_pad))).reshape(depth, 1, V7X_LANES)
    norm_rows = dn_norm_w.reshape(depth, 1, dn_hd)
    gains_mix = ln_mix.reshape(depth, 1, d)
    gains_ffn = ln_ffn.reshape(depth, 1, d)
    final_gain = ln_final.reshape(1, d)
    ba_blk = (used - 2 * dn_heads) // V7X_LANES

    c_pad = jnp.pad(c, ((0, V7X_SUBLANES - batch % V7X_SUBLANES), (0, 0))) if batch % V7X_SUBLANES else c
    mod = _adaln(c_pad, w_ada, b_ada)[:, :batch].reshape(depth, batch, N_MOD, d)
    cos_t, sin_t = _rope_tables(seq)

    x2 = x.reshape(batch * seq, d)
    for l in range(depth):
        proj = _in_proj(x2, gains_mix, mod, w_in_p, l, seq)
        dn = _deltanet(proj, dn_conv_w, alog_rows, dtb_rows, norm_rows, l, batch, seq, dn_heads, dn_hd, ba_blk)
        at = _attention(proj, attn_sinks, cos_t, sin_t, l, batch, seq, dn_width, at_width)
        x2 = _out_proj(dn, at, w_out_p, x2, mod, l, seq)
        x2 = _ffn(x2, gains_ffn, mod, w_gu, w_dn, final_gain, l, seq, final_norm=(l == depth - 1))
    return x2.reshape(batch, seq, d)
```

```python
import functools

import jax
import jax.numpy as jnp
from jax import lax
from jax.experimental import pallas as pl
from jax.experimental.pallas import tpu as pltpu

F32 = jnp.float32
BF16 = jnp.bfloat16

AT_HEAD_DIM = 64
AT_KV_HEADS = 2
CHUNK = 64
WINDOW = 128
ROPE_THETA = 10000.0
EPS = 1e-6
N_MOD = 6

V7X_LANES = 128
V7X_SUBLANES = 8
V7X_VMEM_BYTES = 64 * 1024 * 1024
VMEM_REQUEST_CAP = 60000 * 1024


def _cparams(semantics, vmem_estimate_bytes):
    limit = min(max(int(vmem_estimate_bytes * 1.25), 16 * 1024 * 1024), VMEM_REQUEST_CAP)
    return pltpu.CompilerParams(dimension_semantics=semantics, vmem_limit_bytes=limit)


def _dot(a, b):
    return jnp.dot(a, b, preferred_element_type=F32)


def _dot_nt(a, b):
    return lax.dot_general(a, b, (((1,), (1,)), ((), ())), preferred_element_type=F32)


def _dot_tn(a, b):
    return lax.dot_general(a, b, (((0,), (0,)), ((), ())), preferred_element_type=F32)


def _split3(x):
    x1 = x.astype(BF16)
    r1 = x - x1.astype(F32)
    x2 = r1.astype(BF16)
    x3 = (r1 - x2.astype(F32)).astype(BF16)
    return x1, x2, x3


def _dot_sel(sel_bf16, x):
    x1, x2, x3 = _split3(x)
    return _dot(sel_bf16, x1) + _dot(sel_bf16, x2) + _dot(sel_bf16, x3)


def _div_pow2(x, n):
    shift = n.bit_length() - 1
    assert n == 1 << shift
    return lax.shift_right_arithmetic(x, shift)


def _silu(x):
    return x * jax.nn.sigmoid(x)


def _softplus(x):
    return jnp.maximum(x, 0.0) + jnp.log1p(jnp.exp(-jnp.abs(x)))


NORM_ROWS = 128


def _norm_mod_rows(x_ref, h_ref, gain, shift, scale):
    geff = gain * (1.0 + scale)

    def body(i, carry):
        r = pl.ds(pl.multiple_of(i * NORM_ROWS, NORM_ROWS), NORM_ROWS)
        x = x_ref[r, :]
        inv = lax.rsqrt(jnp.mean(x * x, axis=-1, keepdims=True) + EPS)
        h_ref[r, :] = (x * inv * geff + shift).astype(h_ref.dtype)
        return carry

    lax.fori_loop(0, x_ref.shape[0] // NORM_ROWS, body, 0)


def _adaln_kernel(c_ref, w_ref, b_ref, o_ref):
    ca = _silu(c_ref[...]).astype(BF16)
    o_ref[...] = _dot(ca, w_ref[...].astype(BF16)) + b_ref[...]


def _adaln(c_pad, w_ada, b_ada, tn=1024):
    depth, d, n = w_ada.shape
    rows = c_pad.shape[0]
    est = 2 * d * tn * 4 + d * tn * 2 + 4 * rows * (d + tn) * 4
    return pl.pallas_call(
        _adaln_kernel,
        grid=(depth, n // tn),
        in_specs=[
            pl.BlockSpec((rows, d), lambda l, j: (0, 0)),
            pl.BlockSpec((None, d, tn), lambda l, j: (l, 0, j)),
            pl.BlockSpec((None, 1, tn), lambda l, j: (l, 0, j)),
        ],
        out_specs=pl.BlockSpec((None, rows, tn), lambda l, j: (l, 0, j)),
        out_shape=jax.ShapeDtypeStruct((depth, rows, n), F32),
        compiler_params=_cparams(("arbitrary", "arbitrary"), est),
        name="adaln_mod",
    )(c_pad, w_ada, b_ada.reshape(depth, 1, n))


def _causal_conv(x, tail, w, conv_k):
    rows, cols = x.shape
    tiles = rows // V7X_SUBLANES
    ext = jnp.concatenate([tail, x], axis=0).reshape(tiles + 1, V7X_SUBLANES, cols)
    row = lax.broadcasted_iota(jnp.int32, (1, V7X_SUBLANES, cols), 1)
    taps = [jnp.broadcast_to(w[k:k + 1, :], (V7X_SUBLANES, cols))[None] for k in range(conv_k)]
    acc = ext[1:] * taps[conv_k - 1]
    for j in range(1, conv_k):
        rot = pltpu.roll(ext, j, 1)
        acc = acc + jnp.where(row < j, rot[:-1], rot[1:]) * taps[conv_k - 1 - j]
    return acc.reshape(rows, cols)


def _in_proj_kernel(x_ref, g_ref, mod_ref, w_ref, o_ref, h_ref):
    @pl.when(pl.program_id(1) == 0)
    def _():
        _norm_mod_rows(x_ref, h_ref, g_ref[...], mod_ref[0:1, :], mod_ref[1:2, :])

    o_ref[...] = _dot(h_ref[...], w_ref[...])


def _in_proj(x2, gains, mod, w_bf16, layer, tokens_per_batch, tm=1024, tn=512):
    n, d = x2.shape
    ncols = w_bf16.shape[2]
    tm = min(tm, tokens_per_batch)
    tiles_per_batch = tokens_per_batch // tm
    est = 2 * tm * d * 4 + tm * d * 2 + 2 * d * tn * 2 + 2 * tm * tn * 4 + tm * tn * 4
    return pl.pallas_call(
        _in_proj_kernel,
        grid=(n // tm, ncols // tn),
        in_specs=[
            pl.BlockSpec((tm, d), lambda i, j: (i, 0)),
            pl.BlockSpec((None, 1, d), lambda i, j: (layer, 0, 0)),
            pl.BlockSpec((None, None, N_MOD, d), lambda i, j: (layer, i // tiles_per_batch, 0, 0)),
            pl.BlockSpec((None, d, tn), lambda i, j: (layer, 0, j)),
        ],
        out_specs=pl.BlockSpec((tm, tn), lambda i, j: (i, j)),
        out_shape=jax.ShapeDtypeStruct((n, ncols), F32),
        scratch_shapes=[pltpu.VMEM((tm, d), BF16)],
        compiler_params=_cparams(("parallel", "arbitrary"), est),
        name="in_proj",
    )(x2, gains, mod, w_bf16)


GROUP = 4


def _block_diag(x, mask01):
    return jnp.concatenate([x.astype(BF16)] * GROUP, axis=0) * mask01


def _interleave(gens):
    gens = list(gens)
    while gens:
        for gen in list(gens):
            try:
                next(gen)
            except StopIteration:
                gens.remove(gen)


def _dn_kernel(q_ref, k_ref, v_ref, z_ref, ba_ref, cw_ref, alog_ref, dtb_ref, nw_ref, o_ref,
               tail_ref, qs_ref, ks_ref, vs_ref, s_ref, *, tb, heads, hd, conv_k):
    width = heads * hd
    t = pl.program_id(1)

    @pl.when(t == 0)
    def _():
        tail_ref[...] = jnp.zeros_like(tail_ref)
        s_ref[...] = jnp.zeros_like(s_ref)

    for seg, (src, dst) in enumerate(((q_ref, qs_ref), (k_ref, ks_ref), (v_ref, vs_ref))):
        cur = src[...]
        y = _silu(_causal_conv(cur, tail_ref[seg], cw_ref[:, seg * width:(seg + 1) * width], conv_k))
        tail_ref[seg] = cur[tb - V7X_SUBLANES:, :]
        if seg == 2:
            dst[...] = y
        else:
            post = hd ** -0.5 if seg == 0 else 1.0
            for h in range(heads):
                yh = y[:, h * hd:(h + 1) * hd]
                inv = lax.rsqrt(jnp.sum(yh * yh, axis=-1, keepdims=True) + EPS)
                dst[:, h * hd:(h + 1) * hd] = yh * (inv * post) if post != 1.0 else yh * inv

    ba = ba_ref[...]
    lane = lax.broadcasted_iota(jnp.int32, (tb, V7X_LANES), 1)
    beta_all = jax.nn.sigmoid(ba)
    g_all = jnp.where((lane >= heads) & (lane < 2 * heads),
                      -jnp.exp(alog_ref[...]) * _softplus(ba + dtb_ref[...]), 0.0)

    ri = lax.broadcasted_iota(jnp.int32, (tb, tb), 0)
    ci = lax.broadcasted_iota(jnp.int32, (tb, tb), 1)
    same_chunk = _div_pow2(ri, CHUNK) == _div_pow2(ci, CHUNK)
    cum_sel = jnp.where(same_chunk & (ci <= ri), 1.0, 0.0).astype(BF16)
    tot_sel = jnp.where(same_chunk, 1.0, 0.0).astype(BF16)
    gc_all = _dot_sel(cum_sel, g_all)
    gtot_all = _dot_sel(tot_sel, g_all)

    half = V7X_LANES // 2
    step = 2 * heads
    g1, g2, g3 = _split3(gc_all)
    wp = g1.astype(F32) + pltpu.roll(g2.astype(F32), step, 1) + pltpu.roll(g3.astype(F32), 2 * step, 1)
    wq = pltpu.roll(wp, half, 1)
    lu = lane - half - heads
    upper_any = (lu >= 0) & (lu < 3 * step) & ((lu & (step - 1)) < heads)
    u_all = (wp + jnp.where(upper_any, 1.0, 0.0)).astype(BF16)

    n_groups = heads // GROUP
    rows_g = GROUP * CHUNK
    lane_g = lax.broadcasted_iota(jnp.int32, (rows_g, V7X_LANES), 1)
    head_g = _div_pow2(lax.broadcasted_iota(jnp.int32, (rows_g, V7X_LANES), 0), CHUNK)
    pick_lo, pick_hi = [], []
    for g in range(n_groups):
        off = lane_g - heads - g * GROUP - head_g
        pick_lo.append((off == 0) | (off == step) | (off == 2 * step))
        offh = off - half
        pick_hi.append((offh == 0) | (offh == step) | (offh == 2 * step))

    ii = lax.broadcasted_iota(jnp.int32, (CHUNK, GROUP * CHUNK), 0)
    jj = lax.broadcasted_iota(jnp.int32, (CHUNK, GROUP * CHUNK), 1) & (CHUNK - 1)
    lower = ii >= jj
    strict = ii > jj
    eye = jnp.where(ii == jj, 1.0, 0.0).astype(F32)
    i16, j16, i32, j32 = _div_pow2(ii, 16), _div_pow2(jj, 16), _div_pow2(ii, 32), _div_pow2(jj, 32)
    m_diag16 = i16 == j16
    m_off32 = (i32 == j32) & (i16 > j16)
    m_off64 = i32 > j32
    bd_rows = _div_pow2(lax.broadcasted_iota(jnp.int32, (rows_g, GROUP * hd), 0), CHUNK)
    bd_w = jnp.where(_div_pow2(lax.broadcasted_iota(jnp.int32, (rows_g, rows_g), 0), CHUNK)
                     == _div_pow2(lax.broadcasted_iota(jnp.int32, (rows_g, rows_g), 1), CHUNK), 1.0, 0.0).astype(BF16)
    bd_k = jnp.where(bd_rows == _div_pow2(lax.broadcasted_iota(jnp.int32, (rows_g, GROUP * hd), 1), hd),
                     1.0, 0.0).astype(BF16)
    nw = nw_ref[...]
    stash = {}

    def parallel_part(c):
        r = slice(c * CHUNK, (c + 1) * CHUNK)
        qg, kd = [None] * heads, [None] * heads
        a4, rhs, intra = [], [], []
        for g in range(n_groups):
            kbs, qs_, ks_, wrhs = [], [], [], []
            for hh in range(GROUP):
                h = g * GROUP + hh
                hs = slice(h * hd, (h + 1) * hd)
                gcol = gc_all[r, heads + h:heads + h + 1]
                gtot = gtot_all[r, heads + h:heads + h + 1]
                beta = beta_all[r, h:h + 1]
                egc = jnp.exp(gcol)
                q, k, v = qs_ref[r, hs], ks_ref[r, hs], vs_ref[r, hs]
                kb = k * beta
                kbs.append(kb)
                qs_.append(q)
                ks_.append(k)
                wrhs.append(jnp.concatenate([kb * egc, v * beta], axis=1))
                qg[h] = (q * egc).astype(BF16)
                kd[h] = (k * jnp.exp(gtot - gcol)).astype(BF16)
            lhs = jnp.concatenate([jnp.concatenate(kbs, axis=1), jnp.concatenate(qs_, axis=1)], axis=0)
            kq = _dot_nt(lhs.astype(BF16), _block_diag(jnp.concatenate(ks_, axis=1), bd_k))
            wq4 = jnp.concatenate([wq[r]] * GROUP, axis=0)
            vg = jnp.where(pick_hi[g], -wq4, jnp.where(pick_lo[g], 1.0, 0.0)).astype(BF16)
            decay = jnp.exp(jnp.where(lower, _dot_nt(u_all[r], vg), -jnp.inf))
            a4.append(jnp.where(strict, kq[:CHUNK] * decay, 0.0))
            intra.append((kq[CHUNK:] * decay).astype(BF16))
            rhs.append(jnp.concatenate(wrhs, axis=0).astype(BF16))
        yield
        b = [jnp.where(m_diag16, a, 0.0) for a in a4]
        p = [eye - x for x in b]
        b = [_dot(x.astype(BF16), _block_diag(x, bd_w)) for x in b]
        yield
        for _ in range(2):
            pb = [_dot(jnp.concatenate([x, y], axis=0).astype(BF16), _block_diag(y, bd_w)) for x, y in zip(p, b)]
            p = [x + y[:CHUNK] for x, y in zip(p, pb)]
            b = [y[CHUNK:] for y in pb]
            yield
        p = [x + _dot(x.astype(BF16), _block_diag(y, bd_w)) for x, y in zip(p, b)]
        yield
        for m in (m_off32, m_off64):
            tt = [_dot(jnp.where(m, a, 0.0).astype(BF16), _block_diag(x, bd_w)) for a, x in zip(a4, p)]
            yield
            p = [x - _dot(x.astype(BF16), _block_diag(y, bd_w)) for x, y in zip(p, tt)]
            yield
        wu = [_dot(_block_diag(x, bd_w), y) for x, y in zip(p, rhs)]
        w = [wu[h // GROUP][(h % GROUP) * CHUNK:(h % GROUP + 1) * CHUNK, :hd].astype(BF16) for h in range(heads)]
        u = [wu[h // GROUP][(h % GROUP) * CHUNK:(h % GROUP + 1) * CHUNK, hd:] for h in range(heads)]
        r8 = slice(c * CHUNK, c * CHUNK + V7X_SUBLANES)
        gl = jnp.concatenate([jnp.broadcast_to(jnp.exp(gtot_all[r8, heads + h:heads + h + 1]), (V7X_SUBLANES, hd))
                              for h in range(heads)], axis=1)
        stash[c] = (w, u, qg, kd, intra, gl)

    def recurrent_part(c):
        r = slice(c * CHUNK, (c + 1) * CHUNK)
        w, u, qg, kd, intra, gl = stash.pop(c)
        zs = jnp.zeros((hd, hd), BF16)
        zc = jnp.zeros((CHUNK, hd), BF16)
        states, ws = [], []
        for pr in range(heads // 2):
            h0, h1 = 2 * pr, 2 * pr + 1
            s2 = s_ref[pr]
            sb = s2.astype(BF16)
            bds = jnp.concatenate([jnp.concatenate([sb[:, :hd], zs], axis=1),
                                   jnp.concatenate([zs, sb[:, hd:]], axis=1)], axis=0)
            lhs = jnp.concatenate([jnp.concatenate([w[h0], w[h1]], axis=1),
                                   jnp.concatenate([qg[h0], qg[h1]], axis=1)], axis=0)
            states.append(s2)
            ws.append(_dot(lhs, bds))
        yield
        outs = []
        for pr in range(heads // 2):
            h0, h1 = 2 * pr, 2 * pr + 1
            vnb = (jnp.concatenate([u[h0], u[h1]], axis=1) - ws[pr][:CHUNK]).astype(BF16)
            bdv = jnp.concatenate([jnp.concatenate([vnb[:, :hd], zc], axis=1),
                                   jnp.concatenate([zc, vnb[:, hd:]], axis=1)], axis=0)
            g, sub = divmod(pr, GROUP // 2)
            intra2 = intra[g][:, sub * 2 * CHUNK:(sub + 1) * 2 * CHUNK]
            outs.append(ws[pr][CHUNK:] + _dot(intra2, bdv))
            ds = _dot_tn(jnp.concatenate([kd[h0], kd[h1]], axis=0), bdv)
            s_ref[pr] = states[pr] * gl[0:1, pr * 2 * hd:(pr + 1) * 2 * hd] + ds
        yield
        for pr in range(heads // 2):
            for sub in range(2):
                h = 2 * pr + sub
                hs = slice(h * hd, (h + 1) * hd)
                o = outs[pr][:, sub * hd:(sub + 1) * hd]
                on = o * lax.rsqrt(jnp.mean(o * o, axis=-1, keepdims=True) + EPS) * nw
                o_ref[r, hs] = (on * _silu(z_ref[r, hs])).astype(o_ref.dtype)

    def in_order(*gens):
        for gen in gens:
            yield from gen

    n_chunks = tb // CHUNK
    ahead = 2
    _interleave([parallel_part(c) for c in range(min(ahead, n_chunks))])
    for c0 in range(0, n_chunks, ahead):
        nxt = [parallel_part(c) for c in range(c0 + ahead, min(c0 + 2 * ahead, n_chunks))]
        _interleave(nxt + [in_order(*[recurrent_part(c) for c in range(c0, min(c0 + ahead, n_chunks))])])


def _deltanet(proj, conv_w, alog_rows, dtb_rows, norm_w, layer, batch, seq, heads, hd, ba_blk, tb=512):
    width = heads * hd
    conv_k = conv_w.shape[1]
    tb = min(tb, seq)
    assert conv_k - 1 <= V7X_SUBLANES and tb % CHUNK == 0 and seq % tb == 0
    assert heads % GROUP == 0 and 6 * heads + heads <= V7X_LANES // 2
    nt = seq // tb

    def col(cb):
        return lambda b, t: (b * nt + t, cb)

    est = (2 * 4 * tb * width * 4 + 2 * tb * width * 2 + 3 * V7X_SUBLANES * width * 4 + 9 * tb * width * 4
           + heads * hd * hd * 4 + 2 * conv_k * 3 * width * 4 + 8 * tb * tb * 4)
    kern = functools.partial(_dn_kernel, tb=tb, heads=heads, hd=hd, conv_k=conv_k)
    return pl.pallas_call(
        kern,
        grid=(batch, nt),
        in_specs=[
            pl.BlockSpec((tb, width), col(0)),
            pl.BlockSpec((tb, width), col(1)),
            pl.BlockSpec((tb, width), col(2)),
            pl.BlockSpec((tb, width), col(3)),
            pl.BlockSpec((tb, V7X_LANES), col(ba_blk)),
            pl.BlockSpec((None, conv_k, 3 * width), lambda b, t: (layer, 0, 0)),
            pl.BlockSpec((None, 1, V7X_LANES), lambda b, t: (layer, 0, 0)),
            pl.BlockSpec((None, 1, V7X_LANES), lambda b, t: (layer, 0, 0)),
            pl.BlockSpec((None, 1, hd), lambda b, t: (layer, 0, 0)),
        ],
        out_specs=pl.BlockSpec((tb, width), lambda b, t: (b * nt + t, 0)),
        out_shape=jax.ShapeDtypeStruct((batch * seq, width), BF16),
        scratch_shapes=[
            pltpu.VMEM((3, V7X_SUBLANES, width), F32),
            pltpu.VMEM((tb, width), F32),
            pltpu.VMEM((tb, width), F32),
            pltpu.VMEM((tb, width), F32),
            pltpu.VMEM((heads // 2, hd, 2 * hd), F32),
        ],
        compiler_params=_cparams(("parallel", "arbitrary"), est),
        name="gated_deltanet",
    )(proj, proj, proj, proj, proj, conv_w, alog_rows, dtb_rows, norm_w)


def _rope_kernel(f_ref, cos_ref, sin_ref, *, tr):
    pos = lax.broadcasted_iota(jnp.int32, (tr, V7X_LANES), 0) + pl.program_id(0) * tr
    ang = pos.astype(F32) * f_ref[...]
    lane = lax.broadcasted_iota(jnp.int32, (tr, V7X_LANES), 1)
    cos_ref[...] = jnp.cos(ang)
    sin_ref[...] = jnp.where((lane & (AT_HEAD_DIM - 1)) < AT_HEAD_DIM // 2, -jnp.sin(ang), jnp.sin(ang))


def _rope_tables(seq, tr=512):
    half = AT_HEAD_DIM // 2
    inv_freq = ROPE_THETA ** (-jnp.arange(half, dtype=F32) * 2.0 / AT_HEAD_DIM)
    f_row = jnp.tile(inv_freq, V7X_LANES // half).reshape(1, V7X_LANES)
    tr = min(tr, seq)
    return pl.pallas_call(
        functools.partial(_rope_kernel, tr=tr),
        grid=(seq // tr,),
        in_specs=[pl.BlockSpec((1, V7X_LANES), lambda i: (0, 0))],
        out_specs=[pl.BlockSpec((tr, V7X_LANES), lambda i: (i, 0))] * 2,
        out_shape=[jax.ShapeDtypeStruct((seq, V7X_LANES), F32)] * 2,
        name="rope_tables",
    )(f_row)


def _attn_kernel(sink_ref, q_ref, kc_ref, kp_ref, vc_ref, vp_ref, cc_ref, sc_ref, cp_ref, sp_ref,
                 o_ref, *, pairs, layer, tq):
    n = pl.program_id(1)
    w = WINDOW
    half = V7X_LANES // 2
    quarter = half // 2
    cos_c, sin_c = cc_ref[...], sc_ref[...]

    def rope(x, cos, sin):
        ln = lax.broadcasted_iota(jnp.int32, x.shape, 1)
        partner = jnp.where((ln & (half - 1)) < quarter,
                            pltpu.roll(x, V7X_LANES - quarter, 1), pltpu.roll(x, quarter, 1))
        return x * cos + partner * sin

    lane = lax.broadcasted_iota(jnp.int32, (w + tq, V7X_LANES), 1)
    first = lane < half
    kcat = jnp.concatenate([rope(kp_ref[...], cp_ref[...], sp_ref[...]), rope(kc_ref[...], cos_c, sin_c)], axis=0)
    vcat = jnp.concatenate([vp_ref[...], vc_ref[...]], axis=0)
    kswap, vswap = pltpu.roll(kcat, half, 1), pltpu.roll(vcat, half, 1)
    ones_lo, ones_hi = jnp.where(first, 1.0, 0.0), jnp.where(first, 0.0, 1.0)

    def placed(x, xswap, g, s):
        src = x if g == s else xswap
        return jnp.where(first, src, 0.0) if s == 0 else jnp.where(first, 0.0, src)

    k_at = [[placed(kcat, kswap, g, s).astype(BF16) for s in range(2)] for g in range(AT_KV_HEADS)]
    v_at = [[jnp.concatenate([placed(vcat, vswap, g, s), ones_lo if s == 0 else ones_hi], axis=1).astype(BF16)
             for s in range(2)] for g in range(AT_KV_HEADS)]

    r = lax.broadcasted_iota(jnp.int32, (w, 2 * w), 0)
    j = lax.broadcasted_iota(jnp.int32, (w, 2 * w), 1)
    in_band = (j > r) & (j <= r + w)
    out_first = lax.broadcasted_iota(jnp.int32, (w, V7X_LANES), 1) < half
    scale = AT_HEAD_DIM ** -0.5
    cos_q, sin_q = cos_c * scale, sin_c * scale

    gp = pairs // AT_KV_HEADS
    visible_any = jnp.concatenate([in_band] * pairs, axis=0)
    visible_first = jnp.concatenate([in_band & ((n > 0) | (j >= w))] * pairs, axis=0)
    out_first_all = jnp.concatenate([out_first] * pairs, axis=0)
    sink_a = jnp.concatenate([jnp.full((w, V7X_LANES), sink_ref[layer, 2 * p], F32) for p in range(pairs)], axis=0)
    sink_b = jnp.concatenate([jnp.full((w, V7X_LANES), sink_ref[layer, 2 * p + 1], F32) for p in range(pairs)], axis=0)

    def row_max(x, sink):
        return jnp.maximum(jnp.broadcast_to(jnp.max(x, axis=-1, keepdims=True), sink.shape), sink)

    def scores(sub):
        rows = slice(sub * w, (sub + 1) * w)
        keys = slice(sub * w, (sub + 2) * w)
        zq = jnp.zeros((w, V7X_LANES), BF16)
        qs = []
        for p in range(pairs):
            qr = rope(q_ref[rows, p * V7X_LANES:(p + 1) * V7X_LANES], cos_q[rows], sin_q[rows]).astype(BF16)
            qs.append(jnp.concatenate([qr, zq] if p < gp else [zq, qr], axis=1))
        kk = jnp.concatenate([jnp.concatenate([k_at[g][0][keys], k_at[g][1][keys]], axis=0)
                              for g in range(AT_KV_HEADS)], axis=1)
        return _dot_nt(jnp.concatenate(qs, axis=0), kk)

    n_sub = tq // w
    s_next = scores(0)
    for sub in range(n_sub):
        s = s_next
        if sub + 1 < n_sub:
            s_next = scores(sub + 1)
        rows = slice(sub * w, (sub + 1) * w)
        keys = slice(sub * w, (sub + 2) * w)
        visible = visible_first if sub == 0 else visible_any
        sa = jnp.where(visible, s[:, :2 * w], -jnp.inf)
        sb = jnp.where(visible, s[:, 2 * w:], -jnp.inf)
        ma = row_max(sa, sink_a)
        mb = row_max(sb, sink_b)
        pr = jnp.concatenate([jnp.exp(sa - jnp.concatenate([ma, ma], axis=1)),
                              jnp.exp(sb - jnp.concatenate([mb, mb], axis=1))], axis=1).astype(BF16)
        ov = jnp.concatenate(
            [_dot(pr[g * gp * w:(g + 1) * gp * w], jnp.concatenate([v_at[g][0][keys], v_at[g][1][keys]], axis=0))
             for g in range(AT_KV_HEADS)], axis=0)
        den = ov[:, V7X_LANES:] + jnp.where(out_first_all, jnp.exp(sink_a - ma), jnp.exp(sink_b - mb))
        out = (ov[:, :V7X_LANES] / den).astype(o_ref.dtype)
        for p in range(pairs):
            o_ref[rows, p * V7X_LANES:(p + 1) * V7X_LANES] = out[p * w:(p + 1) * w]


def _attention(proj, sinks, cos_t, sin_t, layer, batch, seq, dn_width, at_width, tq=256):
    w = WINDOW
    tq = min(tq, seq)
    assert tq % w == 0 and seq % tq == 0
    nb = seq // tq
    wpb = tq // w
    pairs = at_width // V7X_LANES
    q_blk = (4 * dn_width) // at_width
    k_blk = (4 * dn_width + at_width) // V7X_LANES

    def cur(cb):
        return lambda b, n: (b * nb + n, cb)

    def prev(cb):
        return lambda b, n: (b * nb * wpb + jnp.maximum(n * wpb - 1, 0), cb)

    def prev_t(b, n):
        return (jnp.maximum(n * wpb - 1, 0), 0)

    est = 2 * tq * at_width * 4 + 2 * tq * at_width * 2 + 24 * (tq + w) * V7X_LANES * 4 + 64 * w * w * 4
    return pl.pallas_call(
        functools.partial(_attn_kernel, pairs=pairs, layer=layer, tq=tq),
        grid=(batch, nb),
        in_specs=[
            pl.BlockSpec(memory_space=pltpu.SMEM),
            pl.BlockSpec((tq, at_width), cur(q_blk)),
            pl.BlockSpec((tq, V7X_LANES), cur(k_blk)),
            pl.BlockSpec((w, V7X_LANES), prev(k_blk)),
            pl.BlockSpec((tq, V7X_LANES), cur(k_blk + 1)),
            pl.BlockSpec((w, V7X_LANES), prev(k_blk + 1)),
            pl.BlockSpec((tq, V7X_LANES), lambda b, n: (n, 0)),
            pl.BlockSpec((tq, V7X_LANES), lambda b, n: (n, 0)),
            pl.BlockSpec((w, V7X_LANES), prev_t),
            pl.BlockSpec((w, V7X_LANES), prev_t),
        ],
        out_specs=pl.BlockSpec((tq, at_width), lambda b, n: (b * nb + n, 0)),
        out_shape=jax.ShapeDtypeStruct((batch * seq, at_width), BF16),
        compiler_params=_cparams(("parallel", "arbitrary"), est),
        name="swa_sink_attention",
    )(sinks, proj, proj, proj, proj, proj, cos_t, sin_t, cos_t, sin_t)


def _out_proj_kernel(dn_ref, at_ref, w_ref, x_ref, mod_ref, o_ref, *, dn_width):
    acc = _dot(dn_ref[...], w_ref[:dn_width, :]) + _dot(at_ref[...], w_ref[dn_width:, :])
    o_ref[...] = x_ref[...] + mod_ref[2:3, :] * acc


def _out_proj(dn, at, w_bf16, x2, mod, layer, tokens_per_batch, tm=512):
    n, d = x2.shape
    dnw, atw = dn.shape[1], at.shape[1]
    tm = min(tm, tokens_per_batch)
    tiles_per_batch = tokens_per_batch // tm
    est = 2 * (dnw + atw) * d * 2 + 2 * tm * (dnw + atw) * 2 + 4 * tm * d * 4 + tm * d * 4
    return pl.pallas_call(
        functools.partial(_out_proj_kernel, dn_width=dnw),
        grid=(n // tm,),
        in_specs=[
            pl.BlockSpec((tm, dnw), lambda i: (i, 0)),
            pl.BlockSpec((tm, atw), lambda i: (i, 0)),
            pl.BlockSpec((None, dnw + atw, d), lambda i: (layer, 0, 0)),
            pl.BlockSpec((tm, d), lambda i: (i, 0)),
            pl.BlockSpec((None, None, N_MOD, d), lambda i: (layer, i // tiles_per_batch, 0, 0)),
        ],
        out_specs=pl.BlockSpec((tm, d), lambda i: (i, 0)),
        out_shape=jax.ShapeDtypeStruct((n, d), F32),
        compiler_params=_cparams(("parallel",), est),
        name="out_proj",
    )(dn, at, w_bf16, x2, mod)


FFN_ROWS = 256


def _ffn_kernel(x_ref, g_ref, mod_ref, wg_ref, wu_ref, wd_ref, fg_ref, o_ref, h_ref, *, final_norm):
    f = pl.program_id(1)
    tm = x_ref.shape[0]

    @pl.when(f == 0)
    def _():
        _norm_mod_rows(x_ref, h_ref, g_ref[...], mod_ref[3:4, :], mod_ref[4:5, :])
        o_ref[...] = jnp.zeros_like(o_ref)

    for c in range(tm // FFN_ROWS):
        r = slice(c * FFN_ROWS, (c + 1) * FFN_ROWS)
        h = h_ref[r, :]
        gate = _dot(h, wg_ref[...])
        up = _dot(h, wu_ref[...])
        o_ref[r, :] += _dot((_silu(gate) * up).astype(BF16), wd_ref[...])

    @pl.when(f == pl.num_programs(1) - 1)
    def _():
        gate_f = mod_ref[5:6, :]
        final_gain = fg_ref[...]

        def body(i, carry):
            r = pl.ds(pl.multiple_of(i * NORM_ROWS, NORM_ROWS), NORM_ROWS)
            y = x_ref[r, :] + gate_f * o_ref[r, :]
            if final_norm:
                y = y * lax.rsqrt(jnp.mean(y * y, axis=-1, keepdims=True) + EPS) * final_gain
            o_ref[r, :] = y
            return carry

        lax.fori_loop(0, tm // NORM_ROWS, body, 0)


def _ffn(x2, gains, mod, w_gu_bf16, w_dn_bf16, final_gain, layer, tokens_per_batch, final_norm, tm=1024, tf=512):
    n, d = x2.shape
    ffn = w_dn_bf16.shape[1]
    tm = min(tm, tokens_per_batch)
    assert tm % FFN_ROWS == 0 and tm % NORM_ROWS == 0
    tiles_per_batch = tokens_per_batch // tm
    nf = ffn // tf
    est = 4 * tm * d * 4 + tm * d * 2 + 6 * d * tf * 2 + 6 * FFN_ROWS * tf * 4 + 2 * FFN_ROWS * d * 4
    return pl.pallas_call(
        functools.partial(_ffn_kernel, final_norm=final_norm),
        grid=(n // tm, nf),
        in_specs=[
            pl.BlockSpec((tm, d), lambda i, f: (i, 0)),
            pl.BlockSpec((None, 1, d), lambda i, f: (layer, 0, 0)),
            pl.BlockSpec((None, None, N_MOD, d), lambda i, f: (layer, i // tiles_per_batch, 0, 0)),
            pl.BlockSpec((None, d, tf), lambda i, f: (layer, 0, f)),
            pl.BlockSpec((None, d, tf), lambda i, f: (layer, 0, nf + f)),
            pl.BlockSpec((None, tf, d), lambda i, f: (layer, f, 0)),
            pl.BlockSpec((1, d), lambda i, f: (0, 0)),
        ],
        out_specs=pl.BlockSpec((tm, d), lambda i, f: (i, 0)),
        out_shape=jax.ShapeDtypeStruct((n, d), F32),
        scratch_shapes=[pltpu.VMEM((tm, d), BF16)],
        compiler_params=_cparams(("parallel", "arbitrary"), est),
        name="swiglu_ffn",
    )(x2, gains, mod, w_gu_bf16, w_gu_bf16, w_dn_bf16, final_gain)


def kernel(x, c, ln_mix, ln_ffn, w_ada, b_ada, w_in, dn_conv_w, dn_a_log, dn_dt_bias, dn_norm_w,
           attn_sinks, w_out, w_gate_up, w_down, ln_final):
    batch, seq, d = x.shape
    depth = w_in.shape[0]
    dn_heads = dn_a_log.shape[1]
    dn_hd = dn_norm_w.shape[1]
    dn_width = dn_heads * dn_hd
    at_q_heads = attn_sinks.shape[1]
    at_width = at_q_heads * AT_HEAD_DIM
    kv_width = AT_KV_HEADS * AT_HEAD_DIM
    assert kv_width == V7X_LANES and at_width % V7X_LANES == 0 and seq % WINDOW == 0
    assert w_in.shape[2] == 4 * dn_width + 2 * dn_heads + at_width + 2 * kv_width
    assert 2 * dn_heads <= V7X_LANES

    o_b = 4 * dn_width
    o_q = o_b + 2 * dn_heads
    used = w_in.shape[2]
    in_tile = 512
    ncols = -(-(used - 2 * dn_heads + V7X_LANES) // in_tile) * in_tile
    w_in_p = jnp.concatenate([
        w_in[..., :o_b].astype(BF16),
        w_in[..., o_q:].astype(BF16),
        w_in[..., o_b:o_q].astype(BF16),
        jnp.zeros((depth, d, ncols - used), BF16),
    ], axis=-1)
    w_out_p = w_out.astype(BF16)
    w_gu = w_gate_up.astype(BF16)
    w_dn = w_down.astype(BF16)

    lanes_pad = V7X_LANES - 2 * dn_heads
    alog_rows = jnp.pad(dn_a_log, ((0, 0), (dn_heads, lanes_pad))).reshape(depth, 1, V7X_LANES)
    dtb_rows = jnp.pad(dn_dt_bias, ((0, 0), (dn_heads, lanes_pad))).reshape(depth, 1, V7X_LANES)
    norm_rows = dn_norm_w.reshape(depth, 1, dn_hd)
    gains_mix = ln_mix.reshape(depth, 1, d)
    gains_ffn = ln_ffn.reshape(depth, 1, d)
    final_gain = ln_final.reshape(1, d)
    ba_blk = (used - 2 * dn_heads) // V7X_LANES

    c_pad = jnp.pad(c, ((0, V7X_SUBLANES - batch % V7X_SUBLANES), (0, 0))) if batch % V7X_SUBLANES else c
    mod = _adaln(c_pad, w_ada, b_ada)[:, :batch].reshape(depth, batch, N_MOD, d)
    cos_t, sin_t = _rope_tables(seq)

    x2 = x.reshape(batch * seq, d)
    for l in range(depth):
        proj = _in_proj(x2, gains_mix, mod, w_in_p, l, seq)
        dn = _deltanet(proj, dn_conv_w, alog_rows, dtb_rows, norm_rows, l, batch, seq, dn_heads, dn_hd, ba_blk)
        at = _attention(proj, attn_sinks, cos_t, sin_t, l, batch, seq, dn_width, at_width)
        x2 = _out_proj(dn, at, w_out_p, x2, mod, l, seq)
        x2 = _ffn(x2, gains_ffn, mod, w_gu, w_dn, final_gain, l, seq, final_norm=(l == depth - 1))
    return x2.reshape(batch, seq, d)
```

```python
import functools

import jax
import jax.numpy as jnp
from jax import lax
from jax.experimental import pallas as pl
from jax.experimental.pallas import tpu as pltpu

F32 = jnp.float32
BF16 = jnp.bfloat16

AT_HEAD_DIM = 64
AT_KV_HEADS = 2
CHUNK = 64
WINDOW = 128
ROPE_THETA = 10000.0
EPS = 1e-6
N_MOD = 6

V7X_LANES = 128
V7X_SUBLANES = 8
V7X_VMEM_BYTES = 64 * 1024 * 1024
VMEM_REQUEST_CAP = 60000 * 1024


def _cparams(semantics, vmem_estimate_bytes):
    limit = min(max(int(vmem_estimate_bytes * 1.25), 16 * 1024 * 1024), VMEM_REQUEST_CAP)
    return pltpu.CompilerParams(dimension_semantics=semantics, vmem_limit_bytes=limit)


def _dot(a, b):
    return jnp.dot(a, b, preferred_element_type=F32)


def _dot_nt(a, b):
    return lax.dot_general(a, b, (((1,), (1,)), ((), ())), preferred_element_type=F32)


def _dot_tn(a, b):
    return lax.dot_general(a, b, (((0,), (0,)), ((), ())), preferred_element_type=F32)


def _split3(x):
    x1 = x.astype(BF16)
    r1 = x - x1.astype(F32)
    x2 = r1.astype(BF16)
    x3 = (r1 - x2.astype(F32)).astype(BF16)
    return x1, x2, x3


def _dot_sel(sel_bf16, x):
    x1, x2, x3 = _split3(x)
    return _dot(sel_bf16, x1) + _dot(sel_bf16, x2) + _dot(sel_bf16, x3)


def _div_pow2(x, n):
    shift = n.bit_length() - 1
    assert n == 1 << shift
    return lax.shift_right_arithmetic(x, shift)


def _silu(x):
    return x * jax.nn.sigmoid(x)


def _softplus(x):
    return jnp.maximum(x, 0.0) + jnp.log1p(jnp.exp(-jnp.abs(x)))


NORM_ROWS = 128


def _norm_mod_rows(x_ref, h_ref, gain, shift, scale):
    geff = gain * (1.0 + scale)

    def body(i, carry):
        r = pl.ds(pl.multiple_of(i * NORM_ROWS, NORM_ROWS), NORM_ROWS)
        x = x_ref[r, :]
        inv = lax.rsqrt(jnp.mean(x * x, axis=-1, keepdims=True) + EPS)
        h_ref[r, :] = (x * inv * geff + shift).astype(h_ref.dtype)
        return carry

    lax.fori_loop(0, x_ref.shape[0] // NORM_ROWS, body, 0)


def _adaln_kernel(c_ref, w_ref, b_ref, o_ref):
    ca = _silu(c_ref[...]).astype(BF16)
    o_ref[...] = _dot(ca, w_ref[...].astype(BF16)) + b_ref[...]


def _adaln(c_pad, w_ada, b_ada, tn=1024):
    depth, d, n = w_ada.shape
    rows = c_pad.shape[0]
    est = 2 * d * tn * 4 + d * tn * 2 + 4 * rows * (d + tn) * 4
    return pl.pallas_call(
        _adaln_kernel,
        grid=(depth, n // tn),
        in_specs=[
            pl.BlockSpec((rows, d), lambda l, j: (0, 0)),
            pl.BlockSpec((None, d, tn), lambda l, j: (l, 0, j)),
            pl.BlockSpec((None, 1, tn), lambda l, j: (l, 0, j)),
        ],
        out_specs=pl.BlockSpec((None, rows, tn), lambda l, j: (l, 0, j)),
        out_shape=jax.ShapeDtypeStruct((depth, rows, n), F32),
        compiler_params=_cparams(("arbitrary", "arbitrary"), est),
        name="adaln_mod",
    )(c_pad, w_ada, b_ada.reshape(depth, 1, n))


def _w_in_layout_kernel(w_ref, o_ref, *, o_gates, o_attn):
    rows, used = w_ref.shape
    n_attn = used - o_attn
    o_ref[:, :o_gates] = w_ref[:, :o_gates].astype(BF16)
    o_ref[:, o_gates:o_gates + n_attn] = w_ref[:, o_attn:].astype(BF16)
    lane = lax.broadcasted_iota(jnp.int32, (rows, V7X_LANES), 1)
    gates = jnp.where(lane < o_attn - o_gates, w_ref[:, o_gates:o_gates + V7X_LANES], 0.0)
    o_ref[:, o_gates + n_attn:o_gates + n_attn + V7X_LANES] = gates.astype(BF16)
    rest = o_ref.shape[1] - (o_gates + n_attn + V7X_LANES)
    if rest:
        o_ref[:, o_gates + n_attn + V7X_LANES:] = jnp.zeros((rows, rest), BF16)


def _w_in_layout(w_in, ncols, o_gates, o_attn, tr=256):
    depth, d, used = w_in.shape
    assert o_attn - o_gates <= V7X_LANES and o_gates % V7X_LANES == 0 and d % tr == 0
    assert ncols >= used - (o_attn - o_gates) + V7X_LANES
    est = 2 * tr * used * 4 + 2 * tr * ncols * 2 + 2 * tr * used * 4
    return pl.pallas_call(
        functools.partial(_w_in_layout_kernel, o_gates=o_gates, o_attn=o_attn),
        grid=(depth, d // tr),
        in_specs=[pl.BlockSpec((None, tr, used), lambda l, i: (l, i, 0))],
        out_specs=pl.BlockSpec((None, tr, ncols), lambda l, i: (l, i, 0)),
        out_shape=jax.ShapeDtypeStruct((depth, d, ncols), BF16),
        compiler_params=_cparams(("arbitrary", "arbitrary"), est),
        name="w_in_layout",
    )(w_in)


def _causal_conv(x, tail, w, conv_k):
    rows, cols = x.shape
    tiles = rows // V7X_SUBLANES
    ext = jnp.concatenate([tail, x], axis=0).reshape(tiles + 1, V7X_SUBLANES, cols)
    row = lax.broadcasted_iota(jnp.int32, (1, V7X_SUBLANES, cols), 1)
    taps = [jnp.broadcast_to(w[k:k + 1, :], (V7X_SUBLANES, cols))[None] for k in range(conv_k)]
    acc = ext[1:] * taps[conv_k - 1]
    for j in range(1, conv_k):
        rot = pltpu.roll(ext, j, 1)
        acc = acc + jnp.where(row < j, rot[:-1], rot[1:]) * taps[conv_k - 1 - j]
    return acc.reshape(rows, cols)


def _in_proj_kernel(x_ref, g_ref, mod_ref, w_ref, o_ref, h_ref):
    @pl.when(pl.program_id(1) == 0)
    def _():
        _norm_mod_rows(x_ref, h_ref, g_ref[...], mod_ref[0:1, :], mod_ref[1:2, :])

    o_ref[...] = _dot(h_ref[...], w_ref[...])


def _in_proj(x2, gains, mod, w_bf16, layer, tokens_per_batch, tm=1024, tn=512):
    n, d = x2.shape
    ncols = w_bf16.shape[2]
    tm = min(tm, tokens_per_batch)
    tiles_per_batch = tokens_per_batch // tm
    est = 2 * tm * d * 4 + tm * d * 2 + 2 * d * tn * 2 + 2 * tm * tn * 4 + tm * tn * 4
    return pl.pallas_call(
        _in_proj_kernel,
        grid=(n // tm, ncols // tn),
        in_specs=[
            pl.BlockSpec((tm, d), lambda i, j: (i, 0)),
            pl.BlockSpec((None, 1, d), lambda i, j: (layer, 0, 0)),
            pl.BlockSpec((None, None, N_MOD, d), lambda i, j: (layer, i // tiles_per_batch, 0, 0)),
            pl.BlockSpec((None, d, tn), lambda i, j: (layer, 0, j)),
        ],
        out_specs=pl.BlockSpec((tm, tn), lambda i, j: (i, j)),
        out_shape=jax.ShapeDtypeStruct((n, ncols), F32),
        scratch_shapes=[pltpu.VMEM((tm, d), BF16)],
        compiler_params=_cparams(("parallel", "arbitrary"), est),
        name="in_proj",
    )(x2, gains, mod, w_bf16)


GROUP = 4


def _block_diag(x, mask01):
    return jnp.concatenate([x.astype(BF16)] * GROUP, axis=0) * mask01


def _interleave(gens):
    gens = list(gens)
    while gens:
        for gen in list(gens):
            try:
                next(gen)
            except StopIteration:
                gens.remove(gen)


def _dn_kernel(q_ref, k_ref, v_ref, z_ref, ba_ref, cw_ref, alog_ref, dtb_ref, nw_ref, o_ref,
               tail_ref, qs_ref, ks_ref, vs_ref, s_ref, *, tb, heads, hd, conv_k):
    width = heads * hd
    t = pl.program_id(1)

    @pl.when(t == 0)
    def _():
        tail_ref[...] = jnp.zeros_like(tail_ref)
        s_ref[...] = jnp.zeros_like(s_ref)

    for seg, (src, dst) in enumerate(((q_ref, qs_ref), (k_ref, ks_ref), (v_ref, vs_ref))):
        cur = src[...]
        y = _silu(_causal_conv(cur, tail_ref[seg], cw_ref[:, seg * width:(seg + 1) * width], conv_k))
        tail_ref[seg] = cur[tb - V7X_SUBLANES:, :]
        if seg == 2:
            dst[...] = y
        else:
            post = hd ** -0.5 if seg == 0 else 1.0
            for h in range(heads):
                yh = y[:, h * hd:(h + 1) * hd]
                inv = lax.rsqrt(jnp.sum(yh * yh, axis=-1, keepdims=True) + EPS)
                dst[:, h * hd:(h + 1) * hd] = yh * (inv * post) if post != 1.0 else yh * inv

    ba = ba_ref[...]
    lane = lax.broadcasted_iota(jnp.int32, (tb, V7X_LANES), 1)
    beta_all = jax.nn.sigmoid(ba)
    g_all = jnp.where((lane >= heads) & (lane < 2 * heads),
                      -jnp.exp(alog_ref[...]) * _softplus(ba + dtb_ref[...]), 0.0)

    ri = lax.broadcasted_iota(jnp.int32, (tb, tb), 0)
    ci = lax.broadcasted_iota(jnp.int32, (tb, tb), 1)
    same_chunk = _div_pow2(ri, CHUNK) == _div_pow2(ci, CHUNK)
    cum_sel = jnp.where(same_chunk & (ci <= ri), 1.0, 0.0).astype(BF16)
    tot_sel = jnp.where(same_chunk, 1.0, 0.0).astype(BF16)
    gc_all = _dot_sel(cum_sel, g_all)
    gtot_all = _dot_sel(tot_sel, g_all)

    half = V7X_LANES // 2
    step = 2 * heads
    g1, g2, g3 = _split3(gc_all)
    wp = g1.astype(F32) + pltpu.roll(g2.astype(F32), step, 1) + pltpu.roll(g3.astype(F32), 2 * step, 1)
    wq = pltpu.roll(wp, half, 1)
    lu = lane - half - heads
    upper_any = (lu >= 0) & (lu < 3 * step) & ((lu & (step - 1)) < heads)
    u_all = (wp + jnp.where(upper_any, 1.0, 0.0)).astype(BF16)

    n_groups = heads // GROUP
    rows_g = GROUP * CHUNK
    lane_g = lax.broadcasted_iota(jnp.int32, (rows_g, V7X_LANES), 1)
    head_g = _div_pow2(lax.broadcasted_iota(jnp.int32, (rows_g, V7X_LANES), 0), CHUNK)
    pick_lo, pick_hi = [], []
    for g in range(n_groups):
        off = lane_g - heads - g * GROUP - head_g
        pick_lo.append((off == 0) | (off == step) | (off == 2 * step))
        offh = off - half
        pick_hi.append((offh == 0) | (offh == step) | (offh == 2 * step))

    ii = lax.broadcasted_iota(jnp.int32, (CHUNK, GROUP * CHUNK), 0)
    jj = lax.broadcasted_iota(jnp.int32, (CHUNK, GROUP * CHUNK), 1) & (CHUNK - 1)
    lower = ii >= jj
    strict = ii > jj
    eye = jnp.where(ii == jj, 1.0, 0.0).astype(F32)
    i16, j16, i32, j32 = _div_pow2(ii, 16), _div_pow2(jj, 16), _div_pow2(ii, 32), _div_pow2(jj, 32)
    m_diag16 = i16 == j16
    m_off32 = (i32 == j32) & (i16 > j16)
    m_off64 = i32 > j32
    bd_rows = _div_pow2(lax.broadcasted_iota(jnp.int32, (rows_g, GROUP * hd), 0), CHUNK)
    bd_w = jnp.where(_div_pow2(lax.broadcasted_iota(jnp.int32, (rows_g, rows_g), 0), CHUNK)
                     == _div_pow2(lax.broadcasted_iota(jnp.int32, (rows_g, rows_g), 1), CHUNK), 1.0, 0.0).astype(BF16)
    bd_k = jnp.where(bd_rows == _div_pow2(lax.broadcasted_iota(jnp.int32, (rows_g, GROUP * hd), 1), hd),
                     1.0, 0.0).astype(BF16)
    nw = nw_ref[...]
    stash = {}

    def parallel_part(c):
        r = slice(c * CHUNK, (c + 1) * CHUNK)
        qg, kd = [None] * heads, [None] * heads
        a4, rhs, intra = [], [], []
        for g in range(n_groups):
            kbs, qs_, ks_, wrhs = [], [], [], []
            for hh in range(GROUP):
                h = g * GROUP + hh
                hs = slice(h * hd, (h + 1) * hd)
                gcol = gc_all[r, heads + h:heads + h + 1]
                gtot = gtot_all[r, heads + h:heads + h + 1]
                beta = beta_all[r, h:h + 1]
                egc = jnp.exp(gcol)
                q, k, v = qs_ref[r, hs], ks_ref[r, hs], vs_ref[r, hs]
                kb = k * beta
                kbs.append(kb)
                qs_.append(q)
                ks_.append(k)
                wrhs.append(jnp.concatenate([kb * egc, v * beta], axis=1))
                qg[h] = (q * egc).astype(BF16)
                kd[h] = (k * jnp.exp(gtot - gcol)).astype(BF16)
            lhs = jnp.concatenate([jnp.concatenate(kbs, axis=1), jnp.concatenate(qs_, axis=1)], axis=0)
            kq = _dot_nt(lhs.astype(BF16), _block_diag(jnp.concatenate(ks_, axis=1), bd_k))
            wq4 = jnp.concatenate([wq[r]] * GROUP, axis=0)
            vg = jnp.where(pick_hi[g], -wq4, jnp.where(pick_lo[g], 1.0, 0.0)).astype(BF16)
            decay = jnp.exp(jnp.where(lower, _dot_nt(u_all[r], vg), -jnp.inf))
            a4.append(jnp.where(strict, kq[:CHUNK] * decay, 0.0))
            intra.append((kq[CHUNK:] * decay).astype(BF16))
            rhs.append(jnp.concatenate(wrhs, axis=0).astype(BF16))
        yield
        b = [jnp.where(m_diag16, a, 0.0) for a in a4]
        p = [eye - x for x in b]
        b = [_dot(x.astype(BF16), _block_diag(x, bd_w)) for x in b]
        yield
        for _ in range(2):
            pb = [_dot(jnp.concatenate([x, y], axis=0).astype(BF16), _block_diag(y, bd_w)) for x, y in zip(p, b)]
            p = [x + y[:CHUNK] for x, y in zip(p, pb)]
            b = [y[CHUNK:] for y in pb]
            yield
        p = [x + _dot(x.astype(BF16), _block_diag(y, bd_w)) for x, y in zip(p, b)]
        yield
        for m in (m_off32, m_off64):
            tt = [_dot(jnp.where(m, a, 0.0).astype(BF16), _block_diag(x, bd_w)) for a, x in zip(a4, p)]
            yield
            p = [x - _dot(x.astype(BF16), _block_diag(y, bd_w)) for x, y in zip(p, tt)]
            yield
        wu = [_dot(_block_diag(x, bd_w), y) for x, y in zip(p, rhs)]
        w = [wu[h // GROUP][(h % GROUP) * CHUNK:(h % GROUP + 1) * CHUNK, :hd].astype(BF16) for h in range(heads)]
        u = [wu[h // GROUP][(h % GROUP) * CHUNK:(h % GROUP + 1) * CHUNK, hd:] for h in range(heads)]
        r8 = slice(c * CHUNK, c * CHUNK + V7X_SUBLANES)
        gl = jnp.concatenate([jnp.broadcast_to(jnp.exp(gtot_all[r8, heads + h:heads + h + 1]), (V7X_SUBLANES, hd))
                              for h in range(heads)], axis=1)
        stash[c] = (w, u, qg, kd, intra, gl)

    def recurrent_part(c):
        r = slice(c * CHUNK, (c + 1) * CHUNK)
        w, u, qg, kd, intra, gl = stash.pop(c)
        zs = jnp.zeros((hd, hd), BF16)
        zc = jnp.zeros((CHUNK, hd), BF16)
        states, ws = [], []
        for pr in range(heads // 2):
            h0, h1 = 2 * pr, 2 * pr + 1
            s2 = s_ref[pr]
            sb = s2.astype(BF16)
            bds = jnp.concatenate([jnp.concatenate([sb[:, :hd], zs], axis=1),
                                   jnp.concatenate([zs, sb[:, hd:]], axis=1)], axis=0)
            lhs = jnp.concatenate([jnp.concatenate([w[h0], w[h1]], axis=1),
                                   jnp.concatenate([qg[h0], qg[h1]], axis=1)], axis=0)
            states.append(s2)
            ws.append(_dot(lhs, bds))
        yield
        outs = []
        for pr in range(heads // 2):
            h0, h1 = 2 * pr, 2 * pr + 1
            vnb = (jnp.concatenate([u[h0], u[h1]], axis=1) - ws[pr][:CHUNK]).astype(BF16)
            bdv = jnp.concatenate([jnp.concatenate([vnb[:, :hd], zc], axis=1),
                                   jnp.concatenate([zc, vnb[:, hd:]], axis=1)], axis=0)
            g, sub = divmod(pr, GROUP // 2)
            intra2 = intra[g][:, sub * 2 * CHUNK:(sub + 1) * 2 * CHUNK]
            outs.append(ws[pr][CHUNK:] + _dot(intra2, bdv))
            ds = _dot_tn(jnp.concatenate([kd[h0], kd[h1]], axis=0), bdv)
            s_ref[pr] = states[pr] * gl[0:1, pr * 2 * hd:(pr + 1) * 2 * hd] + ds
        yield
        for pr in range(heads // 2):
            for sub in range(2):
                h = 2 * pr + sub
                hs = slice(h * hd, (h + 1) * hd)
                o = outs[pr][:, sub * hd:(sub + 1) * hd]
                on = o * lax.rsqrt(jnp.mean(o * o, axis=-1, keepdims=True) + EPS) * nw
                o_ref[r, hs] = (on * _silu(z_ref[r, hs])).astype(o_ref.dtype)

    def in_order(*gens):
        for gen in gens:
            yield from gen

    n_chunks = tb // CHUNK
    ahead = 4
    _interleave([parallel_part(c) for c in range(min(ahead, n_chunks))])
    for c0 in range(0, n_chunks, ahead):
        nxt = [parallel_part(c) for c in range(c0 + ahead, min(c0 + 2 * ahead, n_chunks))]
        _interleave(nxt + [in_order(*[recurrent_part(c) for c in range(c0, min(c0 + ahead, n_chunks))])])


def _deltanet(proj, conv_w, alog_rows, dtb_rows, norm_w, layer, batch, seq, heads, hd, ba_blk, tb=512):
    width = heads * hd
    conv_k = conv_w.shape[1]
    tb = min(tb, seq)
    assert conv_k - 1 <= V7X_SUBLANES and tb % CHUNK == 0 and seq % tb == 0
    assert heads % GROUP == 0 and 6 * heads + heads <= V7X_LANES // 2
    nt = seq // tb

    def col(cb):
        return lambda b, t: (b * nt + t, cb)

    est = (2 * 4 * tb * width * 4 + 2 * tb * width * 2 + 3 * V7X_SUBLANES * width * 4 + 9 * tb * width * 4
           + heads * hd * hd * 4 + 2 * conv_k * 3 * width * 4 + 8 * tb * tb * 4)
    kern = functools.partial(_dn_kernel, tb=tb, heads=heads, hd=hd, conv_k=conv_k)
    return pl.pallas_call(
        kern,
        grid=(batch, nt),
        in_specs=[
            pl.BlockSpec((tb, width), col(0)),
            pl.BlockSpec((tb, width), col(1)),
            pl.BlockSpec((tb, width), col(2)),
            pl.BlockSpec((tb, width), col(3)),
            pl.BlockSpec((tb, V7X_LANES), col(ba_blk)),
            pl.BlockSpec((None, conv_k, 3 * width), lambda b, t: (layer, 0, 0)),
            pl.BlockSpec((None, 1, V7X_LANES), lambda b, t: (layer, 0, 0)),
            pl.BlockSpec((None, 1, V7X_LANES), lambda b, t: (layer, 0, 0)),
            pl.BlockSpec((None, 1, hd), lambda b, t: (layer, 0, 0)),
        ],
        out_specs=pl.BlockSpec((tb, width), lambda b, t: (b * nt + t, 0)),
        out_shape=jax.ShapeDtypeStruct((batch * seq, width), BF16),
        scratch_shapes=[
            pltpu.VMEM((3, V7X_SUBLANES, width), F32),
            pltpu.VMEM((tb, width), F32),
            pltpu.VMEM((tb, width), F32),
            pltpu.VMEM((tb, width), F32),
            pltpu.VMEM((heads // 2, hd, 2 * hd), F32),
        ],
        compiler_params=_cparams(("parallel", "arbitrary"), est),
        name="gated_deltanet",
    )(proj, proj, proj, proj, proj, conv_w, alog_rows, dtb_rows, norm_w)


def _rope_kernel(f_ref, cos_ref, sin_ref, *, tr):
    pos = lax.broadcasted_iota(jnp.int32, (tr, V7X_LANES), 0) + pl.program_id(0) * tr
    ang = pos.astype(F32) * f_ref[...]
    lane = lax.broadcasted_iota(jnp.int32, (tr, V7X_LANES), 1)
    cos_ref[...] = jnp.cos(ang)
    sin_ref[...] = jnp.where((lane & (AT_HEAD_DIM - 1)) < AT_HEAD_DIM // 2, -jnp.sin(ang), jnp.sin(ang))


def _rope_tables(seq, tr=512):
    half = AT_HEAD_DIM // 2
    inv_freq = ROPE_THETA ** (-jnp.arange(half, dtype=F32) * 2.0 / AT_HEAD_DIM)
    f_row = jnp.tile(inv_freq, V7X_LANES // half).reshape(1, V7X_LANES)
    tr = min(tr, seq)
    return pl.pallas_call(
        functools.partial(_rope_kernel, tr=tr),
        grid=(seq // tr,),
        in_specs=[pl.BlockSpec((1, V7X_LANES), lambda i: (0, 0))],
        out_specs=[pl.BlockSpec((tr, V7X_LANES), lambda i: (i, 0))] * 2,
        out_shape=[jax.ShapeDtypeStruct((seq, V7X_LANES), F32)] * 2,
        name="rope_tables",
    )(f_row)


def _attn_kernel(sink_ref, q_ref, kc_ref, kp_ref, vc_ref, vp_ref, cc_ref, sc_ref, cp_ref, sp_ref,
                 o_ref, *, pairs, layer, tq):
    n = pl.program_id(1)
    w = WINDOW
    half = V7X_LANES // 2
    quarter = half // 2
    cos_c, sin_c = cc_ref[...], sc_ref[...]

    def rope(x, cos, sin):
        ln = lax.broadcasted_iota(jnp.int32, x.shape, 1)
        partner = jnp.where((ln & (half - 1)) < quarter,
                            pltpu.roll(x, V7X_LANES - quarter, 1), pltpu.roll(x, quarter, 1))
        return x * cos + partner * sin

    lane = lax.broadcasted_iota(jnp.int32, (w + tq, V7X_LANES), 1)
    first = lane < half
    kcat = jnp.concatenate([rope(kp_ref[...], cp_ref[...], sp_ref[...]), rope(kc_ref[...], cos_c, sin_c)], axis=0)
    vcat = jnp.concatenate([vp_ref[...], vc_ref[...]], axis=0)
    kswap, vswap = pltpu.roll(kcat, half, 1), pltpu.roll(vcat, half, 1)
    ones_lo, ones_hi = jnp.where(first, 1.0, 0.0), jnp.where(first, 0.0, 1.0)

    def placed(x, xswap, g, s):
        src = x if g == s else xswap
        return jnp.where(first, src, 0.0) if s == 0 else jnp.where(first, 0.0, src)

    k_at = [[placed(kcat, kswap, g, s).astype(BF16) for s in range(2)] for g in range(AT_KV_HEADS)]
    v_at = [[jnp.concatenate([placed(vcat, vswap, g, s), ones_lo if s == 0 else ones_hi], axis=1).astype(BF16)
             for s in range(2)] for g in range(AT_KV_HEADS)]

    r = lax.broadcasted_iota(jnp.int32, (w, 2 * w), 0)
    j = lax.broadcasted_iota(jnp.int32, (w, 2 * w), 1)
    in_band = (j > r) & (j <= r + w)
    out_first = lax.broadcasted_iota(jnp.int32, (w, V7X_LANES), 1) < half
    scale = AT_HEAD_DIM ** -0.5
    cos_q, sin_q = cos_c * scale, sin_c * scale

    gp = pairs // AT_KV_HEADS
    visible_any = jnp.concatenate([in_band] * pairs, axis=0)
    visible_first = jnp.concatenate([in_band & ((n > 0) | (j >= w))] * pairs, axis=0)
    out_first_all = jnp.concatenate([out_first] * pairs, axis=0)
    sink_a = jnp.concatenate([jnp.full((w, V7X_LANES), sink_ref[layer, 2 * p], F32) for p in range(pairs)], axis=0)
    sink_b = jnp.concatenate([jnp.full((w, V7X_LANES), sink_ref[layer, 2 * p + 1], F32) for p in range(pairs)], axis=0)

    def row_max(x, sink):
        return jnp.maximum(jnp.broadcast_to(jnp.max(x, axis=-1, keepdims=True), sink.shape), sink)

    def scores(sub):
        rows = slice(sub * w, (sub + 1) * w)
        keys = slice(sub * w, (sub + 2) * w)
        zq = jnp.zeros((w, V7X_LANES), BF16)
        qs = []
        for p in range(pairs):
            qr = rope(q_ref[rows, p * V7X_LANES:(p + 1) * V7X_LANES], cos_q[rows], sin_q[rows]).astype(BF16)
            qs.append(jnp.concatenate([qr, zq] if p < gp else [zq, qr], axis=1))
        kk = jnp.concatenate([jnp.concatenate([k_at[g][0][keys], k_at[g][1][keys]], axis=0)
                              for g in range(AT_KV_HEADS)], axis=1)
        return _dot_nt(jnp.concatenate(qs, axis=0), kk)

    n_sub = tq // w
    s_next = scores(0)
    for sub in range(n_sub):
        s = s_next
        if sub + 1 < n_sub:
            s_next = scores(sub + 1)
        rows = slice(sub * w, (sub + 1) * w)
        keys = slice(sub * w, (sub + 2) * w)
        visible = visible_first if sub == 0 else visible_any
        sa = jnp.where(visible, s[:, :2 * w], -jnp.inf)
        sb = jnp.where(visible, s[:, 2 * w:], -jnp.inf)
        ma = row_max(sa, sink_a)
        mb = row_max(sb, sink_b)
        pr = jnp.concatenate([jnp.exp(sa - jnp.concatenate([ma, ma], axis=1)),
                              jnp.exp(sb - jnp.concatenate([mb, mb], axis=1))], axis=1).astype(BF16)
        ov = jnp.concatenate(
            [_dot(pr[g * gp * w:(g + 1) * gp * w], jnp.concatenate([v_at[g][0][keys], v_at[g][1][keys]], axis=0))
             for g in range(AT_KV_HEADS)], axis=0)
        den = ov[:, V7X_LANES:] + jnp.where(out_first_all, jnp.exp(sink_a - ma), jnp.exp(sink_b - mb))
        out = (ov[:, :V7X_LANES] / den).astype(o_ref.dtype)
        for p in range(pairs):
            o_ref[rows, p * V7X_LANES:(p + 1) * V7X_LANES] = out[p * w:(p + 1) * w]


def _attention(proj, sinks, cos_t, sin_t, layer, batch, seq, dn_width, at_width, tq=256):
    w = WINDOW
    tq = min(tq, seq)
    assert tq % w == 0 and seq % tq == 0
    nb = seq // tq
    wpb = tq // w
    pairs = at_width // V7X_LANES
    q_blk = (4 * dn_width) // at_width
    k_blk = (4 * dn_width + at_width) // V7X_LANES

    def cur(cb):
        return lambda b, n: (b * nb + n, cb)

    def prev(cb):
        return lambda b, n: (b * nb * wpb + jnp.maximum(n * wpb - 1, 0), cb)

    def prev_t(b, n):
        return (jnp.maximum(n * wpb - 1, 0), 0)

    est = 2 * tq * at_width * 4 + 2 * tq * at_width * 2 + 24 * (tq + w) * V7X_LANES * 4 + 64 * w * w * 4
    return pl.pallas_call(
        functools.partial(_attn_kernel, pairs=pairs, layer=layer, tq=tq),
        grid=(batch, nb),
        in_specs=[
            pl.BlockSpec(memory_space=pltpu.SMEM),
            pl.BlockSpec((tq, at_width), cur(q_blk)),
            pl.BlockSpec((tq, V7X_LANES), cur(k_blk)),
            pl.BlockSpec((w, V7X_LANES), prev(k_blk)),
            pl.BlockSpec((tq, V7X_LANES), cur(k_blk + 1)),
            pl.BlockSpec((w, V7X_LANES), prev(k_blk + 1)),
            pl.BlockSpec((tq, V7X_LANES), lambda b, n: (n, 0)),
            pl.BlockSpec((tq, V7X_LANES), lambda b, n: (n, 0)),
            pl.BlockSpec((w, V7X_LANES), prev_t),
            pl.BlockSpec((w, V7X_LANES), prev_t),
        ],
        out_specs=pl.BlockSpec((tq, at_width), lambda b, n: (b * nb + n, 0)),
        out_shape=jax.ShapeDtypeStruct((batch * seq, at_width), BF16),
        compiler_params=_cparams(("parallel", "arbitrary"), est),
        name="swa_sink_attention",
    )(sinks, proj, proj, proj, proj, proj, cos_t, sin_t, cos_t, sin_t)


def _out_proj_kernel(dn_ref, at_ref, w_ref, x_ref, mod_ref, o_ref, *, dn_width):
    acc = _dot(dn_ref[...], w_ref[:dn_width, :]) + _dot(at_ref[...], w_ref[dn_width:, :])
    o_ref[...] = x_ref[...] + mod_ref[2:3, :] * acc


def _out_proj(dn, at, w_bf16, x2, mod, layer, tokens_per_batch, tm=512):
    n, d = x2.shape
    dnw, atw = dn.shape[1], at.shape[1]
    tm = min(tm, tokens_per_batch)
    tiles_per_batch = tokens_per_batch // tm
    est = 2 * (dnw + atw) * d * 2 + 2 * tm * (dnw + atw) * 2 + 4 * tm * d * 4 + tm * d * 4
    return pl.pallas_call(
        functools.partial(_out_proj_kernel, dn_width=dnw),
        grid=(n // tm,),
        in_specs=[
            pl.BlockSpec((tm, dnw), lambda i: (i, 0)),
            pl.BlockSpec((tm, atw), lambda i: (i, 0)),
            pl.BlockSpec((None, dnw + atw, d), lambda i: (layer, 0, 0)),
            pl.BlockSpec((tm, d), lambda i: (i, 0)),
            pl.BlockSpec((None, None, N_MOD, d), lambda i: (layer, i // tiles_per_batch, 0, 0)),
        ],
        out_specs=pl.BlockSpec((tm, d), lambda i: (i, 0)),
        out_shape=jax.ShapeDtypeStruct((n, d), F32),
        compiler_params=_cparams(("parallel",), est),
        name="out_proj",
    )(dn, at, w_bf16, x2, mod)


def _ffn_kernel(x_ref, g_ref, mod_ref, wg_ref, wu_ref, wd_ref, fg_ref, o_ref, h_ref, acc_ref, *, final_norm):
    f = pl.program_id(1)

    @pl.when(f == 0)
    def _():
        _norm_mod_rows(x_ref, h_ref, g_ref[...], mod_ref[3:4, :], mod_ref[4:5, :])
        acc_ref[...] = jnp.zeros_like(acc_ref)

    h = h_ref[...]
    gate = _dot(h, wg_ref[...])
    up = _dot(h, wu_ref[...])
    acc_ref[...] += _dot((_silu(gate) * up).astype(BF16), wd_ref[...])

    @pl.when(f == pl.num_programs(1) - 1)
    def _():
        gate_f = mod_ref[5:6, :]
        final_gain = fg_ref[...]

        def body(i, carry):
            r = pl.ds(pl.multiple_of(i * NORM_ROWS, NORM_ROWS), NORM_ROWS)
            y = x_ref[r, :] + gate_f * acc_ref[r, :]
            if final_norm:
                y = y * lax.rsqrt(jnp.mean(y * y, axis=-1, keepdims=True) + EPS) * final_gain
            o_ref[r, :] = y
            return carry

        lax.fori_loop(0, x_ref.shape[0] // NORM_ROWS, body, 0)


def _ffn(x2, gains, mod, w_gu_bf16, w_dn_bf16, final_gain, layer, tokens_per_batch, final_norm, tm=512, tf=512):
    n, d = x2.shape
    ffn = w_dn_bf16.shape[1]
    tm = min(tm, tokens_per_batch)
    tiles_per_batch = tokens_per_batch // tm
    nf = ffn // tf
    est = 4 * tm * d * 4 + tm * d * 4 + tm * d * 2 + 6 * d * tf * 2 + 4 * tm * tf * 4
    return pl.pallas_call(
        functools.partial(_ffn_kernel, final_norm=final_norm),
        grid=(n // tm, nf),
        in_specs=[
            pl.BlockSpec((tm, d), lambda i, f: (i, 0)),
            pl.BlockSpec((None, 1, d), lambda i, f: (layer, 0, 0)),
            pl.BlockSpec((None, None, N_MOD, d), lambda i, f: (layer, i // tiles_per_batch, 0, 0)),
            pl.BlockSpec((None, d, tf), lambda i, f: (layer, 0, f)),
            pl.BlockSpec((None, d, tf), lambda i, f: (layer, 0, nf + f)),
            pl.BlockSpec((None, tf, d), lambda i, f: (layer, f, 0)),
            pl.BlockSpec((1, d), lambda i, f: (0, 0)),
        ],
        out_specs=pl.BlockSpec((tm, d), lambda i, f: (i, 0)),
        out_shape=jax.ShapeDtypeStruct((n, d), F32),
        scratch_shapes=[pltpu.VMEM((tm, d), BF16), pltpu.VMEM((tm, d), F32)],
        compiler_params=_cparams(("parallel", "arbitrary"), est),
        name="swiglu_ffn",
    )(x2, gains, mod, w_gu_bf16, w_gu_bf16, w_dn_bf16, final_gain)


def kernel(x, c, ln_mix, ln_ffn, w_ada, b_ada, w_in, dn_conv_w, dn_a_log, dn_dt_bias, dn_norm_w,
           attn_sinks, w_out, w_gate_up, w_down, ln_final):
    batch, seq, d = x.shape
    depth = w_in.shape[0]
    dn_heads = dn_a_log.shape[1]
    dn_hd = dn_norm_w.shape[1]
    dn_width = dn_heads * dn_hd
    at_q_heads = attn_sinks.shape[1]
    at_width = at_q_heads * AT_HEAD_DIM
    kv_width = AT_KV_HEADS * AT_HEAD_DIM
    assert kv_width == V7X_LANES and at_width % V7X_LANES == 0 and seq % WINDOW == 0
    assert w_in.shape[2] == 4 * dn_width + 2 * dn_heads + at_width + 2 * kv_width
    assert 2 * dn_heads <= V7X_LANES

    o_b = 4 * dn_width
    o_q = o_b + 2 * dn_heads
    used = w_in.shape[2]
    in_tile = 512
    ncols = -(-(used - 2 * dn_heads + V7X_LANES) // in_tile) * in_tile
    w_in_p = _w_in_layout(w_in, ncols, o_b, o_q)
    w_out_p = w_out.astype(BF16)
    w_gu = w_gate_up.astype(BF16)
    w_dn = w_down.astype(BF16)

    lanes_pad = V7X_LANES - 2 * dn_heads
    alog_rows = jnp.pad(dn_a_log, ((0, 0), (dn_heads, lanes_pad))).reshape(depth, 1, V7X_LANES)
    dtb_rows = jnp.pad(dn_dt_bias, ((0, 0), (dn_heads, lanes_pad))).reshape(depth, 1, V7X_LANES)
    norm_rows = dn_norm_w.reshape(depth, 1, dn_hd)
    gains_mix = ln_mix.reshape(depth, 1, d)
    gains_ffn = ln_ffn.reshape(depth, 1, d)
    final_gain = ln_final.reshape(1, d)
    ba_blk = (used - 2 * dn_heads) // V7X_LANES

    c_pad = jnp.pad(c, ((0, V7X_SUBLANES - batch % V7X_SUBLANES), (0, 0))) if batch % V7X_SUBLANES else c
    mod = _adaln(c_pad, w_ada, b_ada)[:, :batch].reshape(depth, batch, N_MOD, d)
    cos_t, sin_t = _rope_tables(seq)

    x2 = x.reshape(batch * seq, d)
    for l in range(depth):
        proj = _in_proj(x2, gains_mix, mod, w_in_p, l, seq)
        dn = _deltanet(proj, dn_conv_w, alog_rows, dtb_rows, norm_rows, l, batch, seq, dn_heads, dn_hd, ba_blk)
        at = _attention(proj, attn_sinks, cos_t, sin_t, l, batch, seq, dn_width, at_width)
        x2 = _out_proj(dn, at, w_out_p, x2, mod, l, seq)
        x2 = _ffn(x2, gains_ffn, mod, w_gu, w_dn, final_gain, l, seq, final_norm=(l == depth - 1))
    return x2.reshape(batch, seq, d)
```

```python
import functools

import jax
import jax.numpy as jnp
from jax import lax
from jax.experimental import pallas as pl
from jax.experimental.pallas import tpu as pltpu

F32 = jnp.float32
BF16 = jnp.bfloat16

AT_HEAD_DIM = 64
AT_KV_HEADS = 2
CHUNK = 64
WINDOW = 128
ROPE_THETA = 10000.0
EPS = 1e-6
N_MOD = 6
LOG2E = 1.4426950408889634

V7X_LANES = 128
V7X_SUBLANES = 8
V7X_VMEM_BYTES = 64 * 1024 * 1024
VMEM_REQUEST_CAP = 60000 * 1024


def _cparams(semantics, vmem_estimate_bytes):
    limit = min(max(int(vmem_estimate_bytes * 1.25), 16 * 1024 * 1024), VMEM_REQUEST_CAP)
    return pltpu.CompilerParams(dimension_semantics=semantics, vmem_limit_bytes=limit)


def _dot(a, b):
    return jnp.dot(a, b, preferred_element_type=F32)


def _dot_nt(a, b):
    return lax.dot_general(a, b, (((1,), (1,)), ((), ())), preferred_element_type=F32)


def _dot_tn(a, b):
    return lax.dot_general(a, b, (((0,), (0,)), ((), ())), preferred_element_type=F32)


def _split3(x):
    x1 = x.astype(BF16)
    r1 = x - x1.astype(F32)
    x2 = r1.astype(BF16)
    x3 = (r1 - x2.astype(F32)).astype(BF16)
    return x1, x2, x3


def _dot_sel(sel_bf16, x):
    x1, x2, x3 = _split3(x)
    return _dot(sel_bf16, x1) + _dot(sel_bf16, x2) + _dot(sel_bf16, x3)


def _div_pow2(x, n):
    shift = n.bit_length() - 1
    assert n == 1 << shift
    return lax.shift_right_arithmetic(x, shift)


def _silu(x):
    return x * jax.nn.sigmoid(x)


def _softplus(x):
    return jnp.maximum(x, 0.0) + jnp.log1p(jnp.exp(-jnp.abs(x)))


NORM_ROWS = 128


def _norm_mod_rows(x_ref, h_ref, gain, shift, scale):
    geff = gain * (1.0 + scale)

    def body(i, carry):
        r = pl.ds(pl.multiple_of(i * NORM_ROWS, NORM_ROWS), NORM_ROWS)
        x = x_ref[r, :]
        inv = lax.rsqrt(jnp.mean(x * x, axis=-1, keepdims=True) + EPS)
        h_ref[r, :] = (x * inv * geff + shift).astype(h_ref.dtype)
        return carry

    lax.fori_loop(0, x_ref.shape[0] // NORM_ROWS, body, 0)


def _adaln_kernel(c_ref, w_ref, b_ref, o_ref):
    ca = _silu(c_ref[...]).astype(BF16)
    o_ref[...] = _dot(ca, w_ref[...].astype(BF16)) + b_ref[...]


def _adaln(c_pad, w_ada, b_ada, tn=1024):
    depth, d, n = w_ada.shape
    rows = c_pad.shape[0]
    est = 2 * d * tn * 4 + d * tn * 2 + 4 * rows * (d + tn) * 4
    return pl.pallas_call(
        _adaln_kernel,
        grid=(depth, n // tn),
        in_specs=[
            pl.BlockSpec((rows, d), lambda l, j: (0, 0)),
            pl.BlockSpec((None, d, tn), lambda l, j: (l, 0, j)),
            pl.BlockSpec((None, 1, tn), lambda l, j: (l, 0, j)),
        ],
        out_specs=pl.BlockSpec((None, rows, tn), lambda l, j: (l, 0, j)),
        out_shape=jax.ShapeDtypeStruct((depth, rows, n), F32),
        compiler_params=_cparams(("arbitrary", "arbitrary"), est),
        name="adaln_mod",
    )(c_pad, w_ada, b_ada.reshape(depth, 1, n))


def _causal_conv(x, tail, w, conv_k):
    rows, cols = x.shape
    tiles = rows // V7X_SUBLANES
    ext = jnp.concatenate([tail, x], axis=0).reshape(tiles + 1, V7X_SUBLANES, cols)
    row = lax.broadcasted_iota(jnp.int32, (1, V7X_SUBLANES, cols), 1)
    taps = [jnp.broadcast_to(w[k:k + 1, :], (V7X_SUBLANES, cols))[None] for k in range(conv_k)]
    acc = ext[1:] * taps[conv_k - 1]
    for j in range(1, conv_k):
        rot = pltpu.roll(ext, j, 1)
        acc = acc + jnp.where(row < j, rot[:-1], rot[1:]) * taps[conv_k - 1 - j]
    return acc.reshape(rows, cols)


def _in_proj_kernel(x_ref, g_ref, mod_ref, w_ref, o_ref, h_ref):
    @pl.when(pl.program_id(1) == 0)
    def _():
        _norm_mod_rows(x_ref, h_ref, g_ref[...], mod_ref[0:1, :], mod_ref[1:2, :])

    o_ref[...] = _dot_nt(h_ref[...], w_ref[...])


def _in_proj(x2, gains, mod, w_t_bf16, layer, tokens_per_batch, tm=1024, tn=512):
    n, d = x2.shape
    ncols = w_t_bf16.shape[1]
    tm = min(tm, tokens_per_batch)
    tiles_per_batch = tokens_per_batch // tm
    est = 2 * tm * d * 4 + tm * d * 2 + 2 * d * tn * 2 + 2 * tm * tn * 4 + tm * tn * 4
    return pl.pallas_call(
        _in_proj_kernel,
        grid=(n // tm, ncols // tn),
        in_specs=[
            pl.BlockSpec((tm, d), lambda i, j: (i, 0)),
            pl.BlockSpec((None, 1, d), lambda i, j: (layer, 0, 0)),
            pl.BlockSpec((None, None, N_MOD, d), lambda i, j: (layer, i // tiles_per_batch, 0, 0)),
            pl.BlockSpec((None, tn, d), lambda i, j: (layer, j, 0)),
        ],
        out_specs=pl.BlockSpec((tm, tn), lambda i, j: (i, j)),
        out_shape=jax.ShapeDtypeStruct((n, ncols), F32),
        scratch_shapes=[pltpu.VMEM((tm, d), BF16)],
        compiler_params=_cparams(("parallel", "arbitrary"), est),
        name="in_proj",
    )(x2, gains, mod, w_t_bf16)


GROUP = 4


def _block_diag(x, mask01):
    return jnp.concatenate([x.astype(BF16)] * GROUP, axis=0) * mask01


def _interleave(gens):
    gens = list(gens)
    while gens:
        for gen in list(gens):
            try:
                next(gen)
            except StopIteration:
                gens.remove(gen)


def _dn_kernel(q_ref, k_ref, v_ref, z_ref, ba_ref, cw_ref, alog_ref, dtb_ref, nw_ref, o_ref,
               tail_ref, qs_ref, ks_ref, vs_ref, s_ref, *, tb, heads, hd, conv_k):
    width = heads * hd
    t = pl.program_id(1)

    @pl.when(t == 0)
    def _():
        tail_ref[...] = jnp.zeros_like(tail_ref)
        s_ref[...] = jnp.zeros_like(s_ref)

    for seg, (src, dst) in enumerate(((q_ref, qs_ref), (k_ref, ks_ref), (v_ref, vs_ref))):
        cur = src[...]
        y = _silu(_causal_conv(cur, tail_ref[seg], cw_ref[:, seg * width:(seg + 1) * width], conv_k))
        tail_ref[seg] = cur[tb - V7X_SUBLANES:, :]
        if seg == 2:
            dst[...] = y
        else:
            post = hd ** -0.5 if seg == 0 else 1.0
            for h in range(heads):
                yh = y[:, h * hd:(h + 1) * hd]
                inv = lax.rsqrt(jnp.sum(yh * yh, axis=-1, keepdims=True) + EPS)
                dst[:, h * hd:(h + 1) * hd] = yh * (inv * post) if post != 1.0 else yh * inv

    ba = ba_ref[...]
    lane = lax.broadcasted_iota(jnp.int32, (tb, V7X_LANES), 1)
    beta_all = jax.nn.sigmoid(ba)
    g_all = jnp.where((lane >= heads) & (lane < 2 * heads),
                      -jnp.exp(alog_ref[...]) * _softplus(ba + dtb_ref[...]), 0.0)

    ri = lax.broadcasted_iota(jnp.int32, (tb, tb), 0)
    ci = lax.broadcasted_iota(jnp.int32, (tb, tb), 1)
    same_chunk = _div_pow2(ri, CHUNK) == _div_pow2(ci, CHUNK)
    cum_sel = jnp.where(same_chunk & (ci <= ri), 1.0, 0.0).astype(BF16)
    tot_sel = jnp.where(same_chunk, 1.0, 0.0).astype(BF16)
    gc_all = _dot_sel(cum_sel, g_all)
    gtot_all = _dot_sel(tot_sel, g_all)

    half = V7X_LANES // 2
    step = 2 * heads
    g1, g2, g3 = _split3(gc_all)
    wp = g1.astype(F32) + pltpu.roll(g2.astype(F32), step, 1) + pltpu.roll(g3.astype(F32), 2 * step, 1)
    wq = pltpu.roll(wp, half, 1)
    lu = lane - half - heads
    upper_any = (lu >= 0) & (lu < 3 * step) & ((lu & (step - 1)) < heads)
    u_all = (wp + jnp.where(upper_any, 1.0, 0.0)).astype(BF16)

    n_groups = heads // GROUP
    rows_g = GROUP * CHUNK
    lane_g = lax.broadcasted_iota(jnp.int32, (rows_g, V7X_LANES), 1)
    head_g = _div_pow2(lax.broadcasted_iota(jnp.int32, (rows_g, V7X_LANES), 0), CHUNK)
    pick_lo, pick_hi = [], []
    for g in range(n_groups):
        off = lane_g - heads - g * GROUP - head_g
        pick_lo.append((off == 0) | (off == step) | (off == 2 * step))
        offh = off - half
        pick_hi.append((offh == 0) | (offh == step) | (offh == 2 * step))

    ii = lax.broadcasted_iota(jnp.int32, (CHUNK, GROUP * CHUNK), 0)
    jj = lax.broadcasted_iota(jnp.int32, (CHUNK, GROUP * CHUNK), 1) & (CHUNK - 1)
    lower = ii >= jj
    strict = ii > jj
    eye = jnp.where(ii == jj, 1.0, 0.0).astype(F32)
    i16, j16, i32, j32 = _div_pow2(ii, 16), _div_pow2(jj, 16), _div_pow2(ii, 32), _div_pow2(jj, 32)
    m_diag16 = i16 == j16
    m_off32 = (i32 == j32) & (i16 > j16)
    m_off64 = i32 > j32
    bd_rows = _div_pow2(lax.broadcasted_iota(jnp.int32, (rows_g, GROUP * hd), 0), CHUNK)
    bd_w = jnp.where(_div_pow2(lax.broadcasted_iota(jnp.int32, (rows_g, rows_g), 0), CHUNK)
                     == _div_pow2(lax.broadcasted_iota(jnp.int32, (rows_g, rows_g), 1), CHUNK), 1.0, 0.0).astype(BF16)
    bd_k = jnp.where(bd_rows == _div_pow2(lax.broadcasted_iota(jnp.int32, (rows_g, GROUP * hd), 1), hd),
                     1.0, 0.0).astype(BF16)
    nw = nw_ref[...]
    stash = {}

    def parallel_part(c):
        r = slice(c * CHUNK, (c + 1) * CHUNK)
        qg, kd = [None] * heads, [None] * heads
        a4, rhs, intra = [], [], []
        for g in range(n_groups):
            kbs, qs_, ks_, wrhs = [], [], [], []
            for hh in range(GROUP):
                h = g * GROUP + hh
                hs = slice(h * hd, (h + 1) * hd)
                gcol = gc_all[r, heads + h:heads + h + 1]
                gtot = gtot_all[r, heads + h:heads + h + 1]
                beta = beta_all[r, h:h + 1]
                egc = jnp.exp(gcol)
                q, k, v = qs_ref[r, hs], ks_ref[r, hs], vs_ref[r, hs]
                kb = k * beta
                kbs.append(kb)
                qs_.append(q)
                ks_.append(k)
                wrhs.append(jnp.concatenate([kb * egc, v * beta], axis=1))
                qg[h] = (q * egc).astype(BF16)
                kd[h] = (k * jnp.exp(gtot - gcol)).astype(BF16)
            lhs = jnp.concatenate([jnp.concatenate(kbs, axis=1), jnp.concatenate(qs_, axis=1)], axis=0)
            kq = _dot_nt(lhs.astype(BF16), _block_diag(jnp.concatenate(ks_, axis=1), bd_k))
            wq4 = jnp.concatenate([wq[r]] * GROUP, axis=0)
            vg = jnp.where(pick_hi[g], -wq4, jnp.where(pick_lo[g], 1.0, 0.0)).astype(BF16)
            decay = jnp.exp(jnp.where(lower, _dot_nt(u_all[r], vg), -jnp.inf))
            a4.append(jnp.where(strict, kq[:CHUNK] * decay, 0.0))
            intra.append((kq[CHUNK:] * decay).astype(BF16))
            rhs.append(jnp.concatenate(wrhs, axis=0).astype(BF16))
        yield
        b = [jnp.where(m_diag16, a, 0.0) for a in a4]
        p = [eye - x for x in b]
        b = [_dot(x.astype(BF16), _block_diag(x, bd_w)) for x in b]
        yield
        for _ in range(2):
            pb = [_dot(jnp.concatenate([x, y], axis=0).astype(BF16), _block_diag(y, bd_w)) for x, y in zip(p, b)]
            p = [x + y[:CHUNK] for x, y in zip(p, pb)]
            b = [y[CHUNK:] for y in pb]
            yield
        p = [x + _dot(x.astype(BF16), _block_diag(y, bd_w)) for x, y in zip(p, b)]
        yield
        for m in (m_off32, m_off64):
            tt = [_dot(jnp.where(m, a, 0.0).astype(BF16), _block_diag(x, bd_w)) for a, x in zip(a4, p)]
            yield
            p = [x - _dot(x.astype(BF16), _block_diag(y, bd_w)) for x, y in zip(p, tt)]
            yield
        wu = [_dot(_block_diag(x, bd_w), y) for x, y in zip(p, rhs)]
        w = [wu[h // GROUP][(h % GROUP) * CHUNK:(h % GROUP + 1) * CHUNK, :hd].astype(BF16) for h in range(heads)]
        u = [wu[h // GROUP][(h % GROUP) * CHUNK:(h % GROUP + 1) * CHUNK, hd:] for h in range(heads)]
        r8 = slice(c * CHUNK, c * CHUNK + V7X_SUBLANES)
        gl = jnp.concatenate([jnp.broadcast_to(jnp.exp(gtot_all[r8, heads + h:heads + h + 1]), (V7X_SUBLANES, hd))
                              for h in range(heads)], axis=1)
        stash[c] = (w, u, qg, kd, intra, gl)

    def recurrent_part(c):
        r = slice(c * CHUNK, (c + 1) * CHUNK)
        w, u, qg, kd, intra, gl = stash.pop(c)
        zs = jnp.zeros((hd, hd), BF16)
        zc = jnp.zeros((CHUNK, hd), BF16)
        states, ws = [], []
        for pr in range(heads // 2):
            h0, h1 = 2 * pr, 2 * pr + 1
            s2 = s_ref[pr]
            sb = s2.astype(BF16)
            bds = jnp.concatenate([jnp.concatenate([sb[:, :hd], zs], axis=1),
                                   jnp.concatenate([zs, sb[:, hd:]], axis=1)], axis=0)
            lhs = jnp.concatenate([jnp.concatenate([w[h0], w[h1]], axis=1),
                                   jnp.concatenate([qg[h0], qg[h1]], axis=1)], axis=0)
            states.append(s2)
            ws.append(_dot(lhs, bds))
        yield
        outs = []
        for pr in range(heads // 2):
            h0, h1 = 2 * pr, 2 * pr + 1
            vnb = (jnp.concatenate([u[h0], u[h1]], axis=1) - ws[pr][:CHUNK]).astype(BF16)
            bdv = jnp.concatenate([jnp.concatenate([vnb[:, :hd], zc], axis=1),
                                   jnp.concatenate([zc, vnb[:, hd:]], axis=1)], axis=0)
            g, sub = divmod(pr, GROUP // 2)
            intra2 = intra[g][:, sub * 2 * CHUNK:(sub + 1) * 2 * CHUNK]
            outs.append(ws[pr][CHUNK:] + _dot(intra2, bdv))
            ds = _dot_tn(jnp.concatenate([kd[h0], kd[h1]], axis=0), bdv)
            s_ref[pr] = states[pr] * gl[0:1, pr * 2 * hd:(pr + 1) * 2 * hd] + ds
        yield
        for pr in range(heads // 2):
            for sub in range(2):
                h = 2 * pr + sub
                hs = slice(h * hd, (h + 1) * hd)
                o = outs[pr][:, sub * hd:(sub + 1) * hd]
                on = o * lax.rsqrt(jnp.mean(o * o, axis=-1, keepdims=True) + EPS) * nw
                o_ref[r, hs] = (on * _silu(z_ref[r, hs])).astype(o_ref.dtype)

    def in_order(*gens):
        for gen in gens:
            yield from gen

    n_chunks = tb // CHUNK
    ahead = 4
    _interleave([parallel_part(c) for c in range(min(ahead, n_chunks))])
    for c0 in range(0, n_chunks, ahead):
        nxt = [parallel_part(c) for c in range(c0 + ahead, min(c0 + 2 * ahead, n_chunks))]
        _interleave(nxt + [in_order(*[recurrent_part(c) for c in range(c0, min(c0 + ahead, n_chunks))])])


def _deltanet(proj, conv_w, alog_rows, dtb_rows, norm_w, layer, batch, seq, heads, hd, ba_blk, tb=512):
    width = heads * hd
    conv_k = conv_w.shape[1]
    tb = min(tb, seq)
    assert conv_k - 1 <= V7X_SUBLANES and tb % CHUNK == 0 and seq % tb == 0
    assert heads % GROUP == 0 and 6 * heads + heads <= V7X_LANES // 2
    nt = seq // tb

    def col(cb):
        return lambda b, t: (b * nt + t, cb)

    est = (2 * 4 * tb * width * 4 + 2 * tb * width * 2 + 3 * V7X_SUBLANES * width * 4 + 9 * tb * width * 4
           + heads * hd * hd * 4 + 2 * conv_k * 3 * width * 4 + 8 * tb * tb * 4)
    kern = functools.partial(_dn_kernel, tb=tb, heads=heads, hd=hd, conv_k=conv_k)
    return pl.pallas_call(
        kern,
        grid=(batch, nt),
        in_specs=[
            pl.BlockSpec((tb, width), col(0)),
            pl.BlockSpec((tb, width), col(1)),
            pl.BlockSpec((tb, width), col(2)),
            pl.BlockSpec((tb, width), col(3)),
            pl.BlockSpec((tb, V7X_LANES), col(ba_blk)),
            pl.BlockSpec((None, conv_k, 3 * width), lambda b, t: (layer, 0, 0)),
            pl.BlockSpec((None, 1, V7X_LANES), lambda b, t: (layer, 0, 0)),
            pl.BlockSpec((None, 1, V7X_LANES), lambda b, t: (layer, 0, 0)),
            pl.BlockSpec((None, 1, hd), lambda b, t: (layer, 0, 0)),
        ],
        out_specs=pl.BlockSpec((tb, width), lambda b, t: (b * nt + t, 0)),
        out_shape=jax.ShapeDtypeStruct((batch * seq, width), BF16),
        scratch_shapes=[
            pltpu.VMEM((3, V7X_SUBLANES, width), F32),
            pltpu.VMEM((tb, width), F32),
            pltpu.VMEM((tb, width), F32),
            pltpu.VMEM((tb, width), F32),
            pltpu.VMEM((heads // 2, hd, 2 * hd), F32),
        ],
        compiler_params=_cparams(("parallel", "arbitrary"), est),
        name="gated_deltanet",
    )(proj, proj, proj, proj, proj, conv_w, alog_rows, dtb_rows, norm_w)


def _rope_kernel(f_ref, cos_ref, sin_ref, *, tr):
    pos = lax.broadcasted_iota(jnp.int32, (tr, V7X_LANES), 0) + pl.program_id(0) * tr
    ang = pos.astype(F32) * f_ref[...]
    lane = lax.broadcasted_iota(jnp.int32, (tr, V7X_LANES), 1)
    cos_ref[...] = jnp.cos(ang)
    sin_ref[...] = jnp.where((lane & (AT_HEAD_DIM - 1)) < AT_HEAD_DIM // 2, -jnp.sin(ang), jnp.sin(ang))


def _rope_tables(seq, tr=512):
    half = AT_HEAD_DIM // 2
    inv_freq = ROPE_THETA ** (-jnp.arange(half, dtype=F32) * 2.0 / AT_HEAD_DIM)
    f_row = jnp.tile(inv_freq, V7X_LANES // half).reshape(1, V7X_LANES)
    tr = min(tr, seq)
    return pl.pallas_call(
        functools.partial(_rope_kernel, tr=tr),
        grid=(seq // tr,),
        in_specs=[pl.BlockSpec((1, V7X_LANES), lambda i: (0, 0))],
        out_specs=[pl.BlockSpec((tr, V7X_LANES), lambda i: (i, 0))] * 2,
        out_shape=[jax.ShapeDtypeStruct((seq, V7X_LANES), F32)] * 2,
        name="rope_tables",
    )(f_row)


def _attn_kernel(sink_ref, q_ref, kc_ref, kp_ref, vc_ref, vp_ref, cc_ref, sc_ref, cp_ref, sp_ref,
                 o_ref, *, pairs, layer, tq):
    n = pl.program_id(1)
    w = WINDOW
    half = V7X_LANES // 2
    quarter = half // 2
    cos_c, sin_c = cc_ref[...], sc_ref[...]

    def rope(x, cos, sin):
        ln = lax.broadcasted_iota(jnp.int32, x.shape, 1)
        partner = jnp.where((ln & (half - 1)) < quarter,
                            pltpu.roll(x, V7X_LANES - quarter, 1), pltpu.roll(x, quarter, 1))
        return x * cos + partner * sin

    lane = lax.broadcasted_iota(jnp.int32, (w + tq, V7X_LANES), 1)
    first = lane < half
    kcat = jnp.concatenate([rope(kp_ref[...], cp_ref[...], sp_ref[...]), rope(kc_ref[...], cos_c, sin_c)], axis=0)
    vcat = jnp.concatenate([vp_ref[...], vc_ref[...]], axis=0)
    kswap, vswap = pltpu.roll(kcat, half, 1), pltpu.roll(vcat, half, 1)
    ones_lo, ones_hi = jnp.where(first, 1.0, 0.0), jnp.where(first, 0.0, 1.0)

    def placed(x, xswap, g, s):
        src = x if g == s else xswap
        return jnp.where(first, src, 0.0) if s == 0 else jnp.where(first, 0.0, src)

    k_at = [[placed(kcat, kswap, g, s).astype(BF16) for s in range(2)] for g in range(AT_KV_HEADS)]
    v_at = [[jnp.concatenate([placed(vcat, vswap, g, s), ones_lo if s == 0 else ones_hi], axis=1).astype(BF16)
             for s in range(2)] for g in range(AT_KV_HEADS)]

    r = lax.broadcasted_iota(jnp.int32, (w, 2 * w), 0)
    j = lax.broadcasted_iota(jnp.int32, (w, 2 * w), 1)
    in_band = (j > r) & (j <= r + w)
    out_first = lax.broadcasted_iota(jnp.int32, (w, V7X_LANES), 1) < half
    scale = AT_HEAD_DIM ** -0.5 * LOG2E
    cos_q, sin_q = cos_c * scale, sin_c * scale

    gp = pairs // AT_KV_HEADS
    visible_any = jnp.concatenate([in_band] * pairs, axis=0)
    visible_first = jnp.concatenate([in_band & ((n > 0) | (j >= w))] * pairs, axis=0)
    out_first_all = jnp.concatenate([out_first] * pairs, axis=0)
    sink_a = jnp.concatenate([jnp.full((w, V7X_LANES), sink_ref[layer, 2 * p] * LOG2E, F32) for p in range(pairs)], axis=0)
    sink_b = jnp.concatenate([jnp.full((w, V7X_LANES), sink_ref[layer, 2 * p + 1] * LOG2E, F32) for p in range(pairs)], axis=0)

    def row_max(x, sink):
        return jnp.maximum(jnp.broadcast_to(jnp.max(x, axis=-1, keepdims=True), sink.shape), sink)

    def scores(sub):
        rows = slice(sub * w, (sub + 1) * w)
        keys = slice(sub * w, (sub + 2) * w)
        zq = jnp.zeros((w, V7X_LANES), BF16)
        qs = []
        for p in range(pairs):
            qr = rope(q_ref[rows, p * V7X_LANES:(p + 1) * V7X_LANES], cos_q[rows], sin_q[rows]).astype(BF16)
            qs.append(jnp.concatenate([qr, zq] if p < gp else [zq, qr], axis=1))
        kk = jnp.concatenate([jnp.concatenate([k_at[g][0][keys], k_at[g][1][keys]], axis=0)
                              for g in range(AT_KV_HEADS)], axis=1)
        return _dot_nt(jnp.concatenate(qs, axis=0), kk)

    n_sub = tq // w
    s_next = scores(0)
    for sub in range(n_sub):
        s = s_next
        if sub + 1 < n_sub:
            s_next = scores(sub + 1)
        rows = slice(sub * w, (sub + 1) * w)
        keys = slice(sub * w, (sub + 2) * w)
        visible = visible_first if sub == 0 else visible_any
        sa = jnp.where(visible, s[:, :2 * w], -jnp.inf)
        sb = jnp.where(visible, s[:, 2 * w:], -jnp.inf)
        ma = row_max(sa, sink_a)
        mb = row_max(sb, sink_b)
        pr = jnp.concatenate([jnp.exp2(sa - jnp.concatenate([ma, ma], axis=1)),
                              jnp.exp2(sb - jnp.concatenate([mb, mb], axis=1))], axis=1).astype(BF16)
        ov = jnp.concatenate(
            [_dot(pr[g * gp * w:(g + 1) * gp * w], jnp.concatenate([v_at[g][0][keys], v_at[g][1][keys]], axis=0))
             for g in range(AT_KV_HEADS)], axis=0)
        den = ov[:, V7X_LANES:] + jnp.where(out_first_all, jnp.exp2(sink_a - ma), jnp.exp2(sink_b - mb))
        out = (ov[:, :V7X_LANES] / den).astype(o_ref.dtype)
        for p in range(pairs):
            o_ref[rows, p * V7X_LANES:(p + 1) * V7X_LANES] = out[p * w:(p + 1) * w]


def _attention(proj, sinks, cos_t, sin_t, layer, batch, seq, dn_width, at_width, tq=512):
    w = WINDOW
    tq = min(tq, seq)
    assert tq % w == 0 and seq % tq == 0
    nb = seq // tq
    wpb = tq // w
    pairs = at_width // V7X_LANES
    q_blk = (4 * dn_width) // at_width
    k_blk = (4 * dn_width + at_width) // V7X_LANES

    def cur(cb):
        return lambda b, n: (b * nb + n, cb)

    def prev(cb):
        return lambda b, n: (b * nb * wpb + jnp.maximum(n * wpb - 1, 0), cb)

    def prev_t(b, n):
        return (jnp.maximum(n * wpb - 1, 0), 0)

    est = 2 * tq * at_width * 4 + 2 * tq * at_width * 2 + 24 * (tq + w) * V7X_LANES * 4 + 64 * w * w * 4
    return pl.pallas_call(
        functools.partial(_attn_kernel, pairs=pairs, layer=layer, tq=tq),
        grid=(batch, nb),
        in_specs=[
            pl.BlockSpec(memory_space=pltpu.SMEM),
            pl.BlockSpec((tq, at_width), cur(q_blk)),
            pl.BlockSpec((tq, V7X_LANES), cur(k_blk)),
            pl.BlockSpec((w, V7X_LANES), prev(k_blk)),
            pl.BlockSpec((tq, V7X_LANES), cur(k_blk + 1)),
            pl.BlockSpec((w, V7X_LANES), prev(k_blk + 1)),
            pl.BlockSpec((tq, V7X_LANES), lambda b, n: (n, 0)),
            pl.BlockSpec((tq, V7X_LANES), lambda b, n: (n, 0)),
            pl.BlockSpec((w, V7X_LANES), prev_t),
            pl.BlockSpec((w, V7X_LANES), prev_t),
        ],
        out_specs=pl.BlockSpec((tq, at_width), lambda b, n: (b * nb + n, 0)),
        out_shape=jax.ShapeDtypeStruct((batch * seq, at_width), BF16),
        compiler_params=_cparams(("parallel", "arbitrary"), est),
        name="swa_sink_attention",
    )(sinks, proj, proj, proj, proj, proj, cos_t, sin_t, cos_t, sin_t)


def _out_proj_kernel(dn_ref, at_ref, w_ref, x_ref, mod_ref, o_ref, *, dn_width):
    acc = _dot(dn_ref[...], w_ref[:dn_width, :]) + _dot(at_ref[...], w_ref[dn_width:, :])
    o_ref[...] = x_ref[...] + mod_ref[2:3, :] * acc


def _out_proj(dn, at, w_bf16, x2, mod, layer, tokens_per_batch, tm=512):
    n, d = x2.shape
    dnw, atw = dn.shape[1], at.shape[1]
    tm = min(tm, tokens_per_batch)
    tiles_per_batch = tokens_per_batch // tm
    est = 2 * (dnw + atw) * d * 2 + 2 * tm * (dnw + atw) * 2 + 4 * tm * d * 4 + tm * d * 4
    return pl.pallas_call(
        functools.partial(_out_proj_kernel, dn_width=dnw),
        grid=(n // tm,),
        in_specs=[
            pl.BlockSpec((tm, dnw), lambda i: (i, 0)),
            pl.BlockSpec((tm, atw), lambda i: (i, 0)),
            pl.BlockSpec((None, dnw + atw, d), lambda i: (layer, 0, 0)),
            pl.BlockSpec((tm, d), lambda i: (i, 0)),
            pl.BlockSpec((None, None, N_MOD, d), lambda i: (layer, i // tiles_per_batch, 0, 0)),
        ],
        out_specs=pl.BlockSpec((tm, d), lambda i: (i, 0)),
        out_shape=jax.ShapeDtypeStruct((n, d), F32),
        compiler_params=_cparams(("parallel",), est),
        name="out_proj",
    )(dn, at, w_bf16, x2, mod)


def _ffn_kernel(x_ref, g_ref, mod_ref, wg_ref, wu_ref, wd_ref, fg_ref, o_ref, h_ref, acc_ref, *, final_norm):
    f = pl.program_id(1)

    @pl.when(f == 0)
    def _():
        _norm_mod_rows(x_ref, h_ref, g_ref[...], mod_ref[3:4, :], mod_ref[4:5, :])
        acc_ref[...] = jnp.zeros_like(acc_ref)

    h = h_ref[...]
    gate = _dot(h, wg_ref[...])
    up = _dot(h, wu_ref[...])
    acc_ref[...] += _dot((_silu(gate) * up).astype(BF16), wd_ref[...])

    @pl.when(f == pl.num_programs(1) - 1)
    def _():
        gate_f = mod_ref[5:6, :]
        final_gain = fg_ref[...]

        def body(i, carry):
            r = pl.ds(pl.multiple_of(i * NORM_ROWS, NORM_ROWS), NORM_ROWS)
            y = x_ref[r, :] + gate_f * acc_ref[r, :]
            if final_norm:
                y = y * lax.rsqrt(jnp.mean(y * y, axis=-1, keepdims=True) + EPS) * final_gain
            o_ref[r, :] = y
            return carry

        lax.fori_loop(0, x_ref.shape[0] // NORM_ROWS, body, 0)


def _ffn(x2, gains, mod, w_gu_bf16, w_dn_bf16, final_gain, layer, tokens_per_batch, final_norm, tm=512, tf=512):
    n, d = x2.shape
    ffn = w_dn_bf16.shape[1]
    tm = min(tm, tokens_per_batch)
    tiles_per_batch = tokens_per_batch // tm
    nf = ffn // tf
    est = 4 * tm * d * 4 + tm * d * 4 + tm * d * 2 + 6 * d * tf * 2 + 4 * tm * tf * 4
    return pl.pallas_call(
        functools.partial(_ffn_kernel, final_norm=final_norm),
        grid=(n // tm, nf),
        in_specs=[
            pl.BlockSpec((tm, d), lambda i, f: (i, 0)),
            pl.BlockSpec((None, 1, d), lambda i, f: (layer, 0, 0)),
            pl.BlockSpec((None, None, N_MOD, d), lambda i, f: (layer, i // tiles_per_batch, 0, 0)),
            pl.BlockSpec((None, d, tf), lambda i, f: (layer, 0, f)),
            pl.BlockSpec((None, d, tf), lambda i, f: (layer, 0, nf + f)),
            pl.BlockSpec((None, tf, d), lambda i, f: (layer, f, 0)),
            pl.BlockSpec((1, d), lambda i, f: (0, 0)),
        ],
        out_specs=pl.BlockSpec((tm, d), lambda i, f: (i, 0)),
        out_shape=jax.ShapeDtypeStruct((n, d), F32),
        scratch_shapes=[pltpu.VMEM((tm, d), BF16), pltpu.VMEM((tm, d), F32)],
        compiler_params=_cparams(("parallel", "arbitrary"), est),
        name="swiglu_ffn",
    )(x2, gains, mod, w_gu_bf16, w_gu_bf16, w_dn_bf16, final_gain)


def kernel(x, c, ln_mix, ln_ffn, w_ada, b_ada, w_in, dn_conv_w, dn_a_log, dn_dt_bias, dn_norm_w,
           attn_sinks, w_out, w_gate_up, w_down, ln_final):
    batch, seq, d = x.shape
    depth = w_in.shape[0]
    dn_heads = dn_a_log.shape[1]
    dn_hd = dn_norm_w.shape[1]
    dn_width = dn_heads * dn_hd
    at_q_heads = attn_sinks.shape[1]
    at_width = at_q_heads * AT_HEAD_DIM
    kv_width = AT_KV_HEADS * AT_HEAD_DIM
    assert kv_width == V7X_LANES and at_width % V7X_LANES == 0 and seq % WINDOW == 0
    assert w_in.shape[2] == 4 * dn_width + 2 * dn_heads + at_width + 2 * kv_width
    assert 2 * dn_heads <= V7X_LANES

    o_b = 4 * dn_width
    o_q = o_b + 2 * dn_heads
    used = w_in.shape[2]
    in_tile = 512
    ncols = -(-(used - 2 * dn_heads + V7X_LANES) // in_tile) * in_tile
    w_in_t = jnp.swapaxes(w_in, 1, 2)
    w_in_p = jnp.concatenate([
        w_in_t[:, :o_b].astype(BF16),
        w_in_t[:, o_q:].astype(BF16),
        w_in_t[:, o_b:o_q].astype(BF16),
        jnp.zeros((depth, ncols - used, d), BF16),
    ], axis=1)
    w_out_p = w_out.astype(BF16)
    w_gu = w_gate_up.astype(BF16)
    w_dn = w_down.astype(BF16)

    lanes_pad = V7X_LANES - 2 * dn_heads
    alog_rows = jnp.pad(dn_a_log, ((0, 0), (dn_heads, lanes_pad))).reshape(depth, 1, V7X_LANES)
    dtb_rows = jnp.pad(dn_dt_bias, ((0, 0), (dn_heads, lanes_pad))).reshape(depth, 1, V7X_LANES)
    norm_rows = dn_norm_w.reshape(depth, 1, dn_hd)
    gains_mix = ln_mix.reshape(depth, 1, d)
    gains_ffn = ln_ffn.reshape(depth, 1, d)
    final_gain = ln_final.reshape(1, d)
    ba_blk = (used - 2 * dn_heads) // V7X_LANES

    c_pad = jnp.pad(c, ((0, V7X_SUBLANES - batch % V7X_SUBLANES), (0, 0))) if batch % V7X_SUBLANES else c
    mod = _adaln(c_pad, w_ada, b_ada)[:, :batch].reshape(depth, batch, N_MOD, d)
    cos_t, sin_t = _rope_tables(seq)

    x2 = x.reshape(batch * seq, d)
    for l in range(depth):
        proj = _in_proj(x2, gains_mix, mod, w_in_p, l, seq)
        dn = _deltanet(proj, dn_conv_w, alog_rows, dtb_rows, norm_rows, l, batch, seq, dn_heads, dn_hd, ba_blk)
        at = _attention(proj, attn_sinks, cos_t, sin_t, l, batch, seq, dn_width, at_width)
        x2 = _out_proj(dn, at, w_out_p, x2, mod, l, seq)
        x2 = _ffn(x2, gains_ffn, mod, w_gu, w_dn, final_gain, l, seq, final_norm=(l == depth - 1))
    return x2.reshape(batch, seq, d)
```

```python
import functools

import jax
import jax.numpy as jnp
from jax import lax
from jax.experimental import pallas as pl
from jax.experimental.pallas import tpu as pltpu

F32 = jnp.float32
BF16 = jnp.bfloat16

AT_HEAD_DIM = 64
AT_KV_HEADS = 2
CHUNK = 64
WINDOW = 128
ROPE_THETA = 10000.0
EPS = 1e-6
N_MOD = 6
LOG2E = 1.4426950408889634

V7X_LANES = 128
V7X_SUBLANES = 8
V7X_VMEM_BYTES = 64 * 1024 * 1024
VMEM_REQUEST_CAP = 60000 * 1024


def _cparams(semantics, vmem_estimate_bytes):
    limit = min(max(int(vmem_estimate_bytes * 1.25), 16 * 1024 * 1024), VMEM_REQUEST_CAP)
    return pltpu.CompilerParams(dimension_semantics=semantics, vmem_limit_bytes=limit)


def _dot(a, b):
    return jnp.dot(a, b, preferred_element_type=F32)


def _dot_nt(a, b):
    return lax.dot_general(a, b, (((1,), (1,)), ((), ())), preferred_element_type=F32)


def _dot_tn(a, b):
    return lax.dot_general(a, b, (((0,), (0,)), ((), ())), preferred_element_type=F32)


def _split3(x):
    x1 = x.astype(BF16)
    r1 = x - x1.astype(F32)
    x2 = r1.astype(BF16)
    x3 = (r1 - x2.astype(F32)).astype(BF16)
    return x1, x2, x3


def _dot_sel(sel_bf16, x):
    x1, x2, x3 = _split3(x)
    return _dot(sel_bf16, x1) + _dot(sel_bf16, x2) + _dot(sel_bf16, x3)


def _div_pow2(x, n):
    shift = n.bit_length() - 1
    assert n == 1 << shift
    return lax.shift_right_arithmetic(x, shift)


def _silu(x):
    return x * jax.nn.sigmoid(x)


def _softplus(x):
    return jnp.maximum(x, 0.0) + jnp.log1p(jnp.exp(-jnp.abs(x)))


NORM_ROWS = 128


def _norm_mod_rows(x_ref, h_ref, gain, shift, scale):
    geff = gain * (1.0 + scale)

    def body(i, carry):
        r = pl.ds(pl.multiple_of(i * NORM_ROWS, NORM_ROWS), NORM_ROWS)
        x = x_ref[r, :]
        inv = lax.rsqrt(jnp.mean(x * x, axis=-1, keepdims=True) + EPS)
        h_ref[r, :] = (x * inv * geff + shift).astype(h_ref.dtype)
        return carry

    lax.fori_loop(0, x_ref.shape[0] // NORM_ROWS, body, 0)


def _adaln_kernel(c_ref, w_ref, b_ref, o_ref):
    ca = _silu(c_ref[...]).astype(BF16)
    o_ref[...] = _dot(ca, w_ref[...].astype(BF16)) + b_ref[...]


def _adaln(c_pad, w_ada, b_ada, tn=1024):
    depth, d, n = w_ada.shape
    rows = c_pad.shape[0]
    est = 2 * d * tn * 4 + d * tn * 2 + 4 * rows * (d + tn) * 4
    return pl.pallas_call(
        _adaln_kernel,
        grid=(depth, n // tn),
        in_specs=[
            pl.BlockSpec((rows, d), lambda l, j: (0, 0)),
            pl.BlockSpec((None, d, tn), lambda l, j: (l, 0, j)),
            pl.BlockSpec((None, 1, tn), lambda l, j: (l, 0, j)),
        ],
        out_specs=pl.BlockSpec((None, rows, tn), lambda l, j: (l, 0, j)),
        out_shape=jax.ShapeDtypeStruct((depth, rows, n), F32),
        compiler_params=_cparams(("arbitrary", "arbitrary"), est),
        name="adaln_mod",
    )(c_pad, w_ada, b_ada.reshape(depth, 1, n))


def _causal_conv(x, tail, w, conv_k):
    rows, cols = x.shape
    tiles = rows // V7X_SUBLANES
    ext = jnp.concatenate([tail, x], axis=0).reshape(tiles + 1, V7X_SUBLANES, cols)
    row = lax.broadcasted_iota(jnp.int32, (1, V7X_SUBLANES, cols), 1)
    taps = [jnp.broadcast_to(w[k:k + 1, :], (V7X_SUBLANES, cols))[None] for k in range(conv_k)]
    acc = ext[1:] * taps[conv_k - 1]
    for j in range(1, conv_k):
        rot = pltpu.roll(ext, j, 1)
        acc = acc + jnp.where(row < j, rot[:-1], rot[1:]) * taps[conv_k - 1 - j]
    return acc.reshape(rows, cols)


def _in_proj_kernel(x_ref, g_ref, mod_ref, wm_ref, wt_ref, o_ref, h_ref, *, main_tiles):
    j = pl.program_id(1)

    @pl.when(j == 0)
    def _():
        _norm_mod_rows(x_ref, h_ref, g_ref[...], mod_ref[0:1, :], mod_ref[1:2, :])

    @pl.when(j < main_tiles)
    def _():
        o_ref[...] = _dot_nt(h_ref[...], wm_ref[...])

    @pl.when(j >= main_tiles)
    def _():
        o_ref[...] = _dot_nt(h_ref[...], wt_ref[...])


def _in_proj(x2, gains, mod, w_main, w_tail, main_cols, layer, tokens_per_batch, tm=1024, tn=512):
    n, d = x2.shape
    assert main_cols % tn == 0 and w_tail.shape[1] % tn == 0 and w_main.shape[1] >= main_cols
    main_tiles = main_cols // tn
    ncols = main_cols + w_tail.shape[1]
    tm = min(tm, tokens_per_batch)
    tiles_per_batch = tokens_per_batch // tm
    est = 2 * tm * d * 4 + tm * d * 2 + 4 * d * tn * 2 + 2 * tm * tn * 4 + tm * tn * 4
    return pl.pallas_call(
        functools.partial(_in_proj_kernel, main_tiles=main_tiles),
        grid=(n // tm, ncols // tn),
        in_specs=[
            pl.BlockSpec((tm, d), lambda i, j: (i, 0)),
            pl.BlockSpec((None, 1, d), lambda i, j: (layer, 0, 0)),
            pl.BlockSpec((None, None, N_MOD, d), lambda i, j: (layer, i // tiles_per_batch, 0, 0)),
            pl.BlockSpec((None, tn, d), lambda i, j: (layer, jnp.minimum(j, main_tiles - 1), 0)),
            pl.BlockSpec((None, tn, d), lambda i, j: (layer, jnp.maximum(j - main_tiles, 0), 0)),
        ],
        out_specs=pl.BlockSpec((tm, tn), lambda i, j: (i, j)),
        out_shape=jax.ShapeDtypeStruct((n, ncols), F32),
        scratch_shapes=[pltpu.VMEM((tm, d), BF16)],
        compiler_params=_cparams(("parallel", "arbitrary"), est),
        name="in_proj",
    )(x2, gains, mod, w_main, w_tail)


GROUP = 4


def _block_diag(x, mask01):
    return jnp.concatenate([x.astype(BF16)] * GROUP, axis=0) * mask01


def _interleave(gens):
    gens = list(gens)
    while gens:
        for gen in list(gens):
            try:
                next(gen)
            except StopIteration:
                gens.remove(gen)


def _dn_kernel(q_ref, k_ref, v_ref, z_ref, ba_ref, cw_ref, alog_ref, dtb_ref, nw_ref, o_ref,
               tail_ref, qs_ref, ks_ref, vs_ref, s_ref, *, tb, heads, hd, conv_k):
    width = heads * hd
    t = pl.program_id(1)

    @pl.when(t == 0)
    def _():
        tail_ref[...] = jnp.zeros_like(tail_ref)
        s_ref[...] = jnp.zeros_like(s_ref)

    for seg, (src, dst) in enumerate(((q_ref, qs_ref), (k_ref, ks_ref), (v_ref, vs_ref))):
        cur = src[...]
        y = _silu(_causal_conv(cur, tail_ref[seg], cw_ref[:, seg * width:(seg + 1) * width], conv_k))
        tail_ref[seg] = cur[tb - V7X_SUBLANES:, :]
        if seg == 2:
            dst[...] = y
        else:
            post = hd ** -0.5 if seg == 0 else 1.0
            for h in range(heads):
                yh = y[:, h * hd:(h + 1) * hd]
                inv = lax.rsqrt(jnp.sum(yh * yh, axis=-1, keepdims=True) + EPS)
                dst[:, h * hd:(h + 1) * hd] = yh * (inv * post) if post != 1.0 else yh * inv

    ba = ba_ref[...]
    lane = lax.broadcasted_iota(jnp.int32, (tb, V7X_LANES), 1)
    beta_all = jax.nn.sigmoid(ba)
    g_all = jnp.where((lane >= heads) & (lane < 2 * heads),
                      -jnp.exp(alog_ref[...]) * _softplus(ba + dtb_ref[...]), 0.0)

    ri = lax.broadcasted_iota(jnp.int32, (tb, tb), 0)
    ci = lax.broadcasted_iota(jnp.int32, (tb, tb), 1)
    same_chunk = _div_pow2(ri, CHUNK) == _div_pow2(ci, CHUNK)
    cum_sel = jnp.where(same_chunk & (ci <= ri), 1.0, 0.0).astype(BF16)
    tot_sel = jnp.where(same_chunk, 1.0, 0.0).astype(BF16)
    gc_all = _dot_sel(cum_sel, g_all)
    gtot_all = _dot_sel(tot_sel, g_all)

    half = V7X_LANES // 2
    step = 2 * heads
    g1, g2, g3 = _split3(gc_all)
    wp = g1.astype(F32) + pltpu.roll(g2.astype(F32), step, 1) + pltpu.roll(g3.astype(F32), 2 * step, 1)
    wq = pltpu.roll(wp, half, 1)
    lu = lane - half - heads
    upper_any = (lu >= 0) & (lu < 3 * step) & ((lu & (step - 1)) < heads)
    u_all = (wp + jnp.where(upper_any, 1.0, 0.0)).astype(BF16)

    n_groups = heads // GROUP
    rows_g = GROUP * CHUNK
    lane_g = lax.broadcasted_iota(jnp.int32, (rows_g, V7X_LANES), 1)
    head_g = _div_pow2(lax.broadcasted_iota(jnp.int32, (rows_g, V7X_LANES), 0), CHUNK)
    pick_lo, pick_hi = [], []
    for g in range(n_groups):
        off = lane_g - heads - g * GROUP - head_g
        pick_lo.append((off == 0) | (off == step) | (off == 2 * step))
        offh = off - half
        pick_hi.append((offh == 0) | (offh == step) | (offh == 2 * step))

    ii = lax.broadcasted_iota(jnp.int32, (CHUNK, GROUP * CHUNK), 0)
    jj = lax.broadcasted_iota(jnp.int32, (CHUNK, GROUP * CHUNK), 1) & (CHUNK - 1)
    lower = ii >= jj
    strict = ii > jj
    eye = jnp.where(ii == jj, 1.0, 0.0).astype(F32)
    i16, j16, i32, j32 = _div_pow2(ii, 16), _div_pow2(jj, 16), _div_pow2(ii, 32), _div_pow2(jj, 32)
    m_diag16 = i16 == j16
    m_off32 = (i32 == j32) & (i16 > j16)
    m_off64 = i32 > j32
    bd_rows = _div_pow2(lax.broadcasted_iota(jnp.int32, (rows_g, GROUP * hd), 0), CHUNK)
    bd_w = jnp.where(_div_pow2(lax.broadcasted_iota(jnp.int32, (rows_g, rows_g), 0), CHUNK)
                     == _div_pow2(lax.broadcasted_iota(jnp.int32, (rows_g, rows_g), 1), CHUNK), 1.0, 0.0).astype(BF16)
    bd_k = jnp.where(bd_rows == _div_pow2(lax.broadcasted_iota(jnp.int32, (rows_g, GROUP * hd), 1), hd),
                     1.0, 0.0).astype(BF16)
    nw = nw_ref[...]
    stash = {}

    def parallel_part(c):
        r = slice(c * CHUNK, (c + 1) * CHUNK)
        qg, kd = [None] * heads, [None] * heads
        a4, rhs, intra = [], [], []
        for g in range(n_groups):
            kbs, qs_, ks_, wrhs = [], [], [], []
            for hh in range(GROUP):
                h = g * GROUP + hh
                hs = slice(h * hd, (h + 1) * hd)
                gcol = gc_all[r, heads + h:heads + h + 1]
                gtot = gtot_all[r, heads + h:heads + h + 1]
                beta = beta_all[r, h:h + 1]
                egc = jnp.exp(gcol)
                q, k, v = qs_ref[r, hs], ks_ref[r, hs], vs_ref[r, hs]
                kb = k * beta
                kbs.append(kb)
                qs_.append(q)
                ks_.append(k)
                wrhs.append(jnp.concatenate([kb * egc, v * beta], axis=1))
                qg[h] = (q * egc).astype(BF16)
                kd[h] = (k * jnp.exp(gtot - gcol)).astype(BF16)
            lhs = jnp.concatenate([jnp.concatenate(kbs, axis=1), jnp.concatenate(qs_, axis=1)], axis=0)
            kq = _dot_nt(lhs.astype(BF16), _block_diag(jnp.concatenate(ks_, axis=1), bd_k))
            wq4 = jnp.concatenate([wq[r]] * GROUP, axis=0)
            vg = jnp.where(pick_hi[g], -wq4, jnp.where(pick_lo[g], 1.0, 0.0)).astype(BF16)
            decay = jnp.exp(jnp.where(lower, _dot_nt(u_all[r], vg), -jnp.inf))
            a4.append(jnp.where(strict, kq[:CHUNK] * decay, 0.0))
            intra.append((kq[CHUNK:] * decay).astype(BF16))
            rhs.append(jnp.concatenate(wrhs, axis=0).astype(BF16))
        yield
        b = [jnp.where(m_diag16, a, 0.0) for a in a4]
        p = [eye - x for x in b]
        b = [_dot(x.astype(BF16), _block_diag(x, bd_w)) for x in b]
        yield
        for _ in range(2):
            pb = [_dot(jnp.concatenate([x, y], axis=0).astype(BF16), _block_diag(y, bd_w)) for x, y in zip(p, b)]
            p = [x + y[:CHUNK] for x, y in zip(p, pb)]
            b = [y[CHUNK:] for y in pb]
            yield
        p = [x + _dot(x.astype(BF16), _block_diag(y, bd_w)) for x, y in zip(p, b)]
        yield
        for m in (m_off32, m_off64):
            tt = [_dot(jnp.where(m, a, 0.0).astype(BF16), _block_diag(x, bd_w)) for a, x in zip(a4, p)]
            yield
            p = [x - _dot(x.astype(BF16), _block_diag(y, bd_w)) for x, y in zip(p, tt)]
            yield
        wu = [_dot(_block_diag(x, bd_w), y) for x, y in zip(p, rhs)]
        w = [wu[h // GROUP][(h % GROUP) * CHUNK:(h % GROUP + 1) * CHUNK, :hd].astype(BF16) for h in range(heads)]
        u = [wu[h // GROUP][(h % GROUP) * CHUNK:(h % GROUP + 1) * CHUNK, hd:] for h in range(heads)]
        r8 = slice(c * CHUNK, c * CHUNK + V7X_SUBLANES)
        gl = jnp.concatenate([jnp.broadcast_to(jnp.exp(gtot_all[r8, heads + h:heads + h + 1]), (V7X_SUBLANES, hd))
                              for h in range(heads)], axis=1)
        stash[c] = (w, u, qg, kd, intra, gl)

    def recurrent_part(c):
        r = slice(c * CHUNK, (c + 1) * CHUNK)
        w, u, qg, kd, intra, gl = stash.pop(c)
        zs = jnp.zeros((hd, hd), BF16)
        zc = jnp.zeros((CHUNK, hd), BF16)
        states, ws = [], []
        for pr in range(heads // 2):
            h0, h1 = 2 * pr, 2 * pr + 1
            s2 = s_ref[pr]
            sb = s2.astype(BF16)
            bds = jnp.concatenate([jnp.concatenate([sb[:, :hd], zs], axis=1),
                                   jnp.concatenate([zs, sb[:, hd:]], axis=1)], axis=0)
            lhs = jnp.concatenate([jnp.concatenate([w[h0], w[h1]], axis=1),
                                   jnp.concatenate([qg[h0], qg[h1]], axis=1)], axis=0)
            states.append(s2)
            ws.append(_dot(lhs, bds))
        yield
        outs = []
        for pr in range(heads // 2):
            h0, h1 = 2 * pr, 2 * pr + 1
            vnb = (jnp.concatenate([u[h0], u[h1]], axis=1) - ws[pr][:CHUNK]).astype(BF16)
            bdv = jnp.concatenate([jnp.concatenate([vnb[:, :hd], zc], axis=1),
                                   jnp.concatenate([zc, vnb[:, hd:]], axis=1)], axis=0)
            g, sub = divmod(pr, GROUP // 2)
            intra2 = intra[g][:, sub * 2 * CHUNK:(sub + 1) * 2 * CHUNK]
            outs.append(ws[pr][CHUNK:] + _dot(intra2, bdv))
            ds = _dot_tn(jnp.concatenate([kd[h0], kd[h1]], axis=0), bdv)
            s_ref[pr] = states[pr] * gl[0:1, pr * 2 * hd:(pr + 1) * 2 * hd] + ds
        yield
        for pr in range(heads // 2):
            for sub in range(2):
                h = 2 * pr + sub
                hs = slice(h * hd, (h + 1) * hd)
                o = outs[pr][:, sub * hd:(sub + 1) * hd]
                on = o * lax.rsqrt(jnp.mean(o * o, axis=-1, keepdims=True) + EPS) * nw
                o_ref[r, hs] = (on * _silu(z_ref[r, hs])).astype(o_ref.dtype)

    def in_order(*gens):
        for gen in gens:
            yield from gen

    n_chunks = tb // CHUNK
    ahead = 4
    _interleave([parallel_part(c) for c in range(min(ahead, n_chunks))])
    for c0 in range(0, n_chunks, ahead):
        nxt = [parallel_part(c) for c in range(c0 + ahead, min(c0 + 2 * ahead, n_chunks))]
        _interleave(nxt + [in_order(*[recurrent_part(c) for c in range(c0, min(c0 + ahead, n_chunks))])])


def _deltanet(proj, conv_w, alog_rows, dtb_rows, norm_w, layer, batch, seq, heads, hd, ba_blk, tb=512):
    width = heads * hd
    conv_k = conv_w.shape[1]
    tb = min(tb, seq)
    assert conv_k - 1 <= V7X_SUBLANES and tb % CHUNK == 0 and seq % tb == 0
    assert heads % GROUP == 0 and 6 * heads + heads <= V7X_LANES // 2
    nt = seq // tb

    def col(cb):
        return lambda b, t: (b * nt + t, cb)

    est = (2 * 4 * tb * width * 4 + 2 * tb * width * 2 + 3 * V7X_SUBLANES * width * 4 + 9 * tb * width * 4
           + heads * hd * hd * 4 + 2 * conv_k * 3 * width * 4 + 8 * tb * tb * 4)
    kern = functools.partial(_dn_kernel, tb=tb, heads=heads, hd=hd, conv_k=conv_k)
    return pl.pallas_call(
        kern,
        grid=(batch, nt),
        in_specs=[
            pl.BlockSpec((tb, width), col(0)),
            pl.BlockSpec((tb, width), col(1)),
            pl.BlockSpec((tb, width), col(2)),
            pl.BlockSpec((tb, width), col(3)),
            pl.BlockSpec((tb, V7X_LANES), col(ba_blk)),
            pl.BlockSpec((None, conv_k, 3 * width), lambda b, t: (layer, 0, 0)),
            pl.BlockSpec((None, 1, V7X_LANES), lambda b, t: (layer, 0, 0)),
            pl.BlockSpec((None, 1, V7X_LANES), lambda b, t: (layer, 0, 0)),
            pl.BlockSpec((None, 1, hd), lambda b, t: (layer, 0, 0)),
        ],
        out_specs=pl.BlockSpec((tb, width), lambda b, t: (b * nt + t, 0)),
        out_shape=jax.ShapeDtypeStruct((batch * seq, width), BF16),
        scratch_shapes=[
            pltpu.VMEM((3, V7X_SUBLANES, width), F32),
            pltpu.VMEM((tb, width), F32),
            pltpu.VMEM((tb, width), F32),
            pltpu.VMEM((tb, width), F32),
            pltpu.VMEM((heads // 2, hd, 2 * hd), F32),
        ],
        compiler_params=_cparams(("parallel", "arbitrary"), est),
        name="gated_deltanet",
    )(proj, proj, proj, proj, proj, conv_w, alog_rows, dtb_rows, norm_w)


def _rope_kernel(f_ref, cos_ref, sin_ref, *, tr):
    pos = lax.broadcasted_iota(jnp.int32, (tr, V7X_LANES), 0) + pl.program_id(0) * tr
    ang = pos.astype(F32) * f_ref[...]
    lane = lax.broadcasted_iota(jnp.int32, (tr, V7X_LANES), 1)
    cos_ref[...] = jnp.cos(ang)
    sin_ref[...] = jnp.where((lane & (AT_HEAD_DIM - 1)) < AT_HEAD_DIM // 2, -jnp.sin(ang), jnp.sin(ang))


def _rope_tables(seq, tr=512):
    half = AT_HEAD_DIM // 2
    inv_freq = ROPE_THETA ** (-jnp.arange(half, dtype=F32) * 2.0 / AT_HEAD_DIM)
    f_row = jnp.tile(inv_freq, V7X_LANES // half).reshape(1, V7X_LANES)
    tr = min(tr, seq)
    return pl.pallas_call(
        functools.partial(_rope_kernel, tr=tr),
        grid=(seq // tr,),
        in_specs=[pl.BlockSpec((1, V7X_LANES), lambda i: (0, 0))],
        out_specs=[pl.BlockSpec((tr, V7X_LANES), lambda i: (i, 0))] * 2,
        out_shape=[jax.ShapeDtypeStruct((seq, V7X_LANES), F32)] * 2,
        name="rope_tables",
    )(f_row)


def _attn_kernel(sink_ref, q_ref, kc_ref, kp_ref, vc_ref, vp_ref, cc_ref, sc_ref, cp_ref, sp_ref,
                 o_ref, *, pairs, layer, tq):
    n = pl.program_id(1)
    w = WINDOW
    half = V7X_LANES // 2
    quarter = half // 2
    cos_c, sin_c = cc_ref[...], sc_ref[...]

    def rope(x, cos, sin):
        ln = lax.broadcasted_iota(jnp.int32, x.shape, 1)
        partner = jnp.where((ln & (half - 1)) < quarter,
                            pltpu.roll(x, V7X_LANES - quarter, 1), pltpu.roll(x, quarter, 1))
        return x * cos + partner * sin

    lane = lax.broadcasted_iota(jnp.int32, (w + tq, V7X_LANES), 1)
    first = lane < half
    kcat = jnp.concatenate([rope(kp_ref[...], cp_ref[...], sp_ref[...]), rope(kc_ref[...], cos_c, sin_c)], axis=0)
    vcat = jnp.concatenate([vp_ref[...], vc_ref[...]], axis=0)
    kswap, vswap = pltpu.roll(kcat, half, 1), pltpu.roll(vcat, half, 1)
    ones_lo, ones_hi = jnp.where(first, 1.0, 0.0), jnp.where(first, 0.0, 1.0)

    def placed(x, xswap, g, s):
        src = x if g == s else xswap
        return jnp.where(first, src, 0.0) if s == 0 else jnp.where(first, 0.0, src)

    k_at = [[placed(kcat, kswap, g, s).astype(BF16) for s in range(2)] for g in range(AT_KV_HEADS)]
    v_at = [[jnp.concatenate([placed(vcat, vswap, g, s), ones_lo if s == 0 else ones_hi], axis=1).astype(BF16)
             for s in range(2)] for g in range(AT_KV_HEADS)]

    r = lax.broadcasted_iota(jnp.int32, (w, 2 * w), 0)
    j = lax.broadcasted_iota(jnp.int32, (w, 2 * w), 1)
    in_band = (j > r) & (j <= r + w)
    out_first = lax.broadcasted_iota(jnp.int32, (w, V7X_LANES), 1) < half
    scale = AT_HEAD_DIM ** -0.5 * LOG2E
    cos_q, sin_q = cos_c * scale, sin_c * scale

    gp = pairs // AT_KV_HEADS
    visible_any = jnp.concatenate([in_band] * pairs, axis=0)
    visible_first = jnp.concatenate([in_band & ((n > 0) | (j >= w))] * pairs, axis=0)
    out_first_all = jnp.concatenate([out_first] * pairs, axis=0)
    sink_a = jnp.concatenate([jnp.full((w, V7X_LANES), sink_ref[layer, 2 * p] * LOG2E, F32) for p in range(pairs)], axis=0)
    sink_b = jnp.concatenate([jnp.full((w, V7X_LANES), sink_ref[layer, 2 * p + 1] * LOG2E, F32) for p in range(pairs)], axis=0)

    def row_max(x, sink):
        return jnp.maximum(jnp.broadcast_to(jnp.max(x, axis=-1, keepdims=True), sink.shape), sink)

    def scores(sub):
        rows = slice(sub * w, (sub + 1) * w)
        keys = slice(sub * w, (sub + 2) * w)
        zq = jnp.zeros((w, V7X_LANES), BF16)
        qs = []
        for p in range(pairs):
            qr = rope(q_ref[rows, p * V7X_LANES:(p + 1) * V7X_LANES], cos_q[rows], sin_q[rows]).astype(BF16)
            qs.append(jnp.concatenate([qr, zq] if p < gp else [zq, qr], axis=1))
        kk = jnp.concatenate([jnp.concatenate([k_at[g][0][keys], k_at[g][1][keys]], axis=0)
                              for g in range(AT_KV_HEADS)], axis=1)
        return _dot_nt(jnp.concatenate(qs, axis=0), kk)

    n_sub = tq // w
    s_next = scores(0)
    for sub in range(n_sub):
        s = s_next
        if sub + 1 < n_sub:
            s_next = scores(sub + 1)
        rows = slice(sub * w, (sub + 1) * w)
        keys = slice(sub * w, (sub + 2) * w)
        visible = visible_first if sub == 0 else visible_any
        sa = jnp.where(visible, s[:, :2 * w], -jnp.inf)
        sb = jnp.where(visible, s[:, 2 * w:], -jnp.inf)
        ma = row_max(sa, sink_a)
        mb = row_max(sb, sink_b)
        pr = jnp.concatenate([jnp.exp2(sa - jnp.concatenate([ma, ma], axis=1)),
                              jnp.exp2(sb - jnp.concatenate([mb, mb], axis=1))], axis=1).astype(BF16)
        ov = jnp.concatenate(
            [_dot(pr[g * gp * w:(g + 1) * gp * w], jnp.concatenate([v_at[g][0][keys], v_at[g][1][keys]], axis=0))
             for g in range(AT_KV_HEADS)], axis=0)
        den = ov[:, V7X_LANES:] + jnp.where(out_first_all, jnp.exp2(sink_a - ma), jnp.exp2(sink_b - mb))
        out = (ov[:, :V7X_LANES] / den).astype(o_ref.dtype)
        for p in range(pairs):
            o_ref[rows, p * V7X_LANES:(p + 1) * V7X_LANES] = out[p * w:(p + 1) * w]


def _attention(proj, sinks, cos_t, sin_t, layer, batch, seq, dn_width, at_width, tq=512):
    w = WINDOW
    tq = min(tq, seq)
    assert tq % w == 0 and seq % tq == 0
    nb = seq // tq
    wpb = tq // w
    pairs = at_width // V7X_LANES
    q_blk = (4 * dn_width) // at_width
    k_blk = (4 * dn_width + at_width) // V7X_LANES

    def cur(cb):
        return lambda b, n: (b * nb + n, cb)

    def prev(cb):
        return lambda b, n: (b * nb * wpb + jnp.maximum(n * wpb - 1, 0), cb)

    def prev_t(b, n):
        return (jnp.maximum(n * wpb - 1, 0), 0)

    est = 2 * tq * at_width * 4 + 2 * tq * at_width * 2 + 24 * (tq + w) * V7X_LANES * 4 + 64 * w * w * 4
    return pl.pallas_call(
        functools.partial(_attn_kernel, pairs=pairs, layer=layer, tq=tq),
        grid=(batch, nb),
        in_specs=[
            pl.BlockSpec(memory_space=pltpu.SMEM),
            pl.BlockSpec((tq, at_width), cur(q_blk)),
            pl.BlockSpec((tq, V7X_LANES), cur(k_blk)),
            pl.BlockSpec((w, V7X_LANES), prev(k_blk)),
            pl.BlockSpec((tq, V7X_LANES), cur(k_blk + 1)),
            pl.BlockSpec((w, V7X_LANES), prev(k_blk + 1)),
            pl.BlockSpec((tq, V7X_LANES), lambda b, n: (n, 0)),
            pl.BlockSpec((tq, V7X_LANES), lambda b, n: (n, 0)),
            pl.BlockSpec((w, V7X_LANES), prev_t),
            pl.BlockSpec((w, V7X_LANES), prev_t),
        ],
        out_specs=pl.BlockSpec((tq, at_width), lambda b, n: (b * nb + n, 0)),
        out_shape=jax.ShapeDtypeStruct((batch * seq, at_width), BF16),
        compiler_params=_cparams(("parallel", "arbitrary"), est),
        name="swa_sink_attention",
    )(sinks, proj, proj, proj, proj, proj, cos_t, sin_t, cos_t, sin_t)


def _out_proj_kernel(dn_ref, at_ref, w_ref, x_ref, mod_ref, o_ref, *, dn_width):
    acc = _dot(dn_ref[...], w_ref[:dn_width, :]) + _dot(at_ref[...], w_ref[dn_width:, :])
    o_ref[...] = x_ref[...] + mod_ref[2:3, :] * acc


def _out_proj(dn, at, w_bf16, x2, mod, layer, tokens_per_batch, tm=512):
    n, d = x2.shape
    dnw, atw = dn.shape[1], at.shape[1]
    tm = min(tm, tokens_per_batch)
    tiles_per_batch = tokens_per_batch // tm
    est = 2 * (dnw + atw) * d * 2 + 2 * tm * (dnw + atw) * 2 + 4 * tm * d * 4 + tm * d * 4
    return pl.pallas_call(
        functools.partial(_out_proj_kernel, dn_width=dnw),
        grid=(n // tm,),
        in_specs=[
            pl.BlockSpec((tm, dnw), lambda i: (i, 0)),
            pl.BlockSpec((tm, atw), lambda i: (i, 0)),
            pl.BlockSpec((None, dnw + atw, d), lambda i: (layer, 0, 0)),
            pl.BlockSpec((tm, d), lambda i: (i, 0)),
            pl.BlockSpec((None, None, N_MOD, d), lambda i: (layer, i // tiles_per_batch, 0, 0)),
        ],
        out_specs=pl.BlockSpec((tm, d), lambda i: (i, 0)),
        out_shape=jax.ShapeDtypeStruct((n, d), F32),
        compiler_params=_cparams(("parallel",), est),
        name="out_proj",
    )(dn, at, w_bf16, x2, mod)


def _ffn_kernel(x_ref, g_ref, mod_ref, wg_ref, wu_ref, wd_ref, fg_ref, o_ref, h_ref, acc_ref, *, final_norm):
    f = pl.program_id(1)

    @pl.when(f == 0)
    def _():
        _norm_mod_rows(x_ref, h_ref, g_ref[...], mod_ref[3:4, :], mod_ref[4:5, :])
        acc_ref[...] = jnp.zeros_like(acc_ref)

    h = h_ref[...]
    gate = _dot(h, wg_ref[...])
    up = _dot(h, wu_ref[...])
    acc_ref[...] += _dot((_silu(gate) * up).astype(BF16), wd_ref[...])

    @pl.when(f == pl.num_programs(1) - 1)
    def _():
        gate_f = mod_ref[5:6, :]
        final_gain = fg_ref[...]

        def body(i, carry):
            r = pl.ds(pl.multiple_of(i * NORM_ROWS, NORM_ROWS), NORM_ROWS)
            y = x_ref[r, :] + gate_f * acc_ref[r, :]
            if final_norm:
                y = y * lax.rsqrt(jnp.mean(y * y, axis=-1, keepdims=True) + EPS) * final_gain
            o_ref[r, :] = y
            return carry

        lax.fori_loop(0, x_ref.shape[0] // NORM_ROWS, body, 0)


def _ffn(x2, gains, mod, w_gu_bf16, w_dn_bf16, final_gain, layer, tokens_per_batch, final_norm, tm=512, tf=512):
    n, d = x2.shape
    ffn = w_dn_bf16.shape[1]
    tm = min(tm, tokens_per_batch)
    tiles_per_batch = tokens_per_batch // tm
    nf = ffn // tf
    est = 4 * tm * d * 4 + tm * d * 4 + tm * d * 2 + 6 * d * tf * 2 + 4 * tm * tf * 4
    return pl.pallas_call(
        functools.partial(_ffn_kernel, final_norm=final_norm),
        grid=(n // tm, nf),
        in_specs=[
            pl.BlockSpec((tm, d), lambda i, f: (i, 0)),
            pl.BlockSpec((None, 1, d), lambda i, f: (layer, 0, 0)),
            pl.BlockSpec((None, None, N_MOD, d), lambda i, f: (layer, i // tiles_per_batch, 0, 0)),
            pl.BlockSpec((None, d, tf), lambda i, f: (layer, 0, f)),
            pl.BlockSpec((None, d, tf), lambda i, f: (layer, 0, nf + f)),
            pl.BlockSpec((None, tf, d), lambda i, f: (layer, f, 0)),
            pl.BlockSpec((1, d), lambda i, f: (0, 0)),
        ],
        out_specs=pl.BlockSpec((tm, d), lambda i, f: (i, 0)),
        out_shape=jax.ShapeDtypeStruct((n, d), F32),
        scratch_shapes=[pltpu.VMEM((tm, d), BF16), pltpu.VMEM((tm, d), F32)],
        compiler_params=_cparams(("parallel", "arbitrary"), est),
        name="swiglu_ffn",
    )(x2, gains, mod, w_gu_bf16, w_gu_bf16, w_dn_bf16, final_gain)


def kernel(x, c, ln_mix, ln_ffn, w_ada, b_ada, w_in, dn_conv_w, dn_a_log, dn_dt_bias, dn_norm_w,
           attn_sinks, w_out, w_gate_up, w_down, ln_final):
    batch, seq, d = x.shape
    depth = w_in.shape[0]
    dn_heads = dn_a_log.shape[1]
    dn_hd = dn_norm_w.shape[1]
    dn_width = dn_heads * dn_hd
    at_q_heads = attn_sinks.shape[1]
    at_width = at_q_heads * AT_HEAD_DIM
    kv_width = AT_KV_HEADS * AT_HEAD_DIM
    assert kv_width == V7X_LANES and at_width % V7X_LANES == 0 and seq % WINDOW == 0
    assert w_in.shape[2] == 4 * dn_width + 2 * dn_heads + at_width + 2 * kv_width
    assert 2 * dn_heads <= V7X_LANES

    o_b = 4 * dn_width
    o_q = o_b + 2 * dn_heads
    used = w_in.shape[2]
    in_tile = 512
    ncols = -(-(used - 2 * dn_heads + V7X_LANES) // in_tile) * in_tile
    w_in_t = jnp.swapaxes(w_in, 1, 2).astype(BF16)
    w_in_tail = jnp.concatenate([
        w_in_t[:, o_q:],
        w_in_t[:, o_b:o_q],
        jnp.zeros((depth, ncols - used, d), BF16),
    ], axis=1)
    w_out_p = w_out.astype(BF16)
    w_gu = w_gate_up.astype(BF16)
    w_dn = w_down.astype(BF16)

    lanes_pad = V7X_LANES - 2 * dn_heads
    alog_rows = jnp.pad(dn_a_log, ((0, 0), (dn_heads, lanes_pad))).reshape(depth, 1, V7X_LANES)
    dtb_rows = jnp.pad(dn_dt_bias, ((0, 0), (dn_heads, lanes_pad))).reshape(depth, 1, V7X_LANES)
    norm_rows = dn_norm_w.reshape(depth, 1, dn_hd)
    gains_mix = ln_mix.reshape(depth, 1, d)
    gains_ffn = ln_ffn.reshape(depth, 1, d)
    final_gain = ln_final.reshape(1, d)
    ba_blk = (used - 2 * dn_heads) // V7X_LANES

    c_pad = jnp.pad(c, ((0, V7X_SUBLANES - batch % V7X_SUBLANES), (0, 0))) if batch % V7X_SUBLANES else c
    mod = _adaln(c_pad, w_ada, b_ada)[:, :batch].reshape(depth, batch, N_MOD, d)
    cos_t, sin_t = _rope_tables(seq)

    x2 = x.reshape(batch * seq, d)
    for l in range(depth):
        proj = _in_proj(x2, gains_mix, mod, w_in_t, w_in_tail, o_b, l, seq)
        dn = _deltanet(proj, dn_conv_w, alog_rows, dtb_rows, norm_rows, l, batch, seq, dn_heads, dn_hd, ba_blk)
        at = _attention(proj, attn_sinks, cos_t, sin_t, l, batch, seq, dn_width, at_width)
        x2 = _out_proj(dn, at, w_out_p, x2, mod, l, seq)
        x2 = _ffn(x2, gains_ffn, mod, w_gu, w_dn, final_gain, l, seq, final_norm=(l == depth - 1))
    return x2.reshape(batch, seq, d)
```

```python
import functools

import jax
import jax.numpy as jnp
from jax import lax
from jax.experimental import pallas as pl
from jax.experimental.pallas import tpu as pltpu

F32 = jnp.float32
BF16 = jnp.bfloat16

AT_HEAD_DIM = 64
AT_KV_HEADS = 2
CHUNK = 64
WINDOW = 128
ROPE_THETA = 10000.0
EPS = 1e-6
N_MOD = 6
LOG2E = 1.4426950408889634

V7X_LANES = 128
V7X_SUBLANES = 8
V7X_VMEM_BYTES = 64 * 1024 * 1024
VMEM_REQUEST_CAP = 60000 * 1024


def _cparams(semantics, vmem_estimate_bytes):
    limit = min(max(int(vmem_estimate_bytes * 1.25), 16 * 1024 * 1024), VMEM_REQUEST_CAP)
    return pltpu.CompilerParams(dimension_semantics=semantics, vmem_limit_bytes=limit)


def _dot(a, b):
    return jnp.dot(a, b, preferred_element_type=F32)


def _dot_nt(a, b):
    return lax.dot_general(a, b, (((1,), (1,)), ((), ())), preferred_element_type=F32)


def _dot_tn(a, b):
    return lax.dot_general(a, b, (((0,), (0,)), ((), ())), preferred_element_type=F32)


def _split3(x):
    x1 = x.astype(BF16)
    r1 = x - x1.astype(F32)
    x2 = r1.astype(BF16)
    x3 = (r1 - x2.astype(F32)).astype(BF16)
    return x1, x2, x3


def _dot_sel(sel_bf16, x):
    x1, x2, x3 = _split3(x)
    return _dot(sel_bf16, x1) + _dot(sel_bf16, x2) + _dot(sel_bf16, x3)


def _div_pow2(x, n):
    shift = n.bit_length() - 1
    assert n == 1 << shift
    return lax.shift_right_arithmetic(x, shift)


def _silu(x):
    t = 0.5 * x
    return t + t * jnp.tanh(t)


def _softplus(x):
    return jnp.maximum(x, 0.0) + jnp.log1p(jnp.exp(-jnp.abs(x)))


NORM_ROWS = 128


def _norm_mod_rows(x_ref, h_ref, gain, shift, scale):
    geff = gain * (1.0 + scale)

    def body(i, carry):
        r = pl.ds(pl.multiple_of(i * NORM_ROWS, NORM_ROWS), NORM_ROWS)
        x = x_ref[r, :]
        inv = lax.rsqrt(jnp.mean(x * x, axis=-1, keepdims=True) + EPS)
        h_ref[r, :] = (x * inv * geff + shift).astype(h_ref.dtype)
        return carry

    lax.fori_loop(0, x_ref.shape[0] // NORM_ROWS, body, 0)


def _adaln_kernel(c_ref, w_ref, b_ref, o_ref):
    ca = _silu(c_ref[...]).astype(BF16)
    o_ref[...] = _dot(ca, w_ref[...].astype(BF16)) + b_ref[...]


def _adaln(c_pad, w_ada, b_ada, tn=1024):
    depth, d, n = w_ada.shape
    rows = c_pad.shape[0]
    est = 2 * d * tn * 4 + d * tn * 2 + 4 * rows * (d + tn) * 4
    return pl.pallas_call(
        _adaln_kernel,
        grid=(depth, n // tn),
        in_specs=[
            pl.BlockSpec((rows, d), lambda l, j: (0, 0)),
            pl.BlockSpec((None, d, tn), lambda l, j: (l, 0, j)),
            pl.BlockSpec((None, 1, tn), lambda l, j: (l, 0, j)),
        ],
        out_specs=pl.BlockSpec((None, rows, tn), lambda l, j: (l, 0, j)),
        out_shape=jax.ShapeDtypeStruct((depth, rows, n), F32),
        compiler_params=_cparams(("arbitrary", "arbitrary"), est),
        name="adaln_mod",
    )(c_pad, w_ada, b_ada.reshape(depth, 1, n))


def _causal_conv(x, tail, w, conv_k):
    rows, cols = x.shape
    tiles = rows // V7X_SUBLANES
    ext = jnp.concatenate([tail, x], axis=0).reshape(tiles + 1, V7X_SUBLANES, cols)
    row = lax.broadcasted_iota(jnp.int32, (1, V7X_SUBLANES, cols), 1)
    taps = [jnp.broadcast_to(w[k:k + 1, :], (V7X_SUBLANES, cols))[None] for k in range(conv_k)]
    acc = ext[1:] * taps[conv_k - 1]
    for j in range(1, conv_k):
        rot = pltpu.roll(ext, j, 1)
        acc = acc + jnp.where(row < j, rot[:-1], rot[1:]) * taps[conv_k - 1 - j]
    return acc.reshape(rows, cols)


def _in_proj_kernel(x_ref, g_ref, mod_ref, wm_ref, wt_ref, o_ref, h_ref, *, main_tiles):
    j = pl.program_id(1)

    @pl.when(j == 0)
    def _():
        _norm_mod_rows(x_ref, h_ref, g_ref[...], mod_ref[0:1, :], mod_ref[1:2, :])

    @pl.when(j < main_tiles)
    def _():
        o_ref[...] = _dot_nt(h_ref[...], wm_ref[...])

    @pl.when(j >= main_tiles)
    def _():
        o_ref[...] = _dot_nt(h_ref[...], wt_ref[...])


def _in_proj(x2, gains, mod, w_main, w_tail, main_cols, layer, tokens_per_batch, tm=1024, tn=512):
    n, d = x2.shape
    assert main_cols % tn == 0 and w_tail.shape[1] % tn == 0 and w_main.shape[1] >= main_cols
    main_tiles = main_cols // tn
    ncols = main_cols + w_tail.shape[1]
    tm = min(tm, tokens_per_batch)
    tiles_per_batch = tokens_per_batch // tm
    est = 2 * tm * d * 4 + tm * d * 2 + 4 * d * tn * 2 + 2 * tm * tn * 4 + tm * tn * 4
    return pl.pallas_call(
        functools.partial(_in_proj_kernel, main_tiles=main_tiles),
        grid=(n // tm, ncols // tn),
        in_specs=[
            pl.BlockSpec((tm, d), lambda i, j: (i, 0)),
            pl.BlockSpec((None, 1, d), lambda i, j: (layer, 0, 0)),
            pl.BlockSpec((None, None, N_MOD, d), lambda i, j: (layer, i // tiles_per_batch, 0, 0)),
            pl.BlockSpec((None, tn, d), lambda i, j: (layer, jnp.minimum(j, main_tiles - 1), 0)),
            pl.BlockSpec((None, tn, d), lambda i, j: (layer, jnp.maximum(j - main_tiles, 0), 0)),
        ],
        out_specs=pl.BlockSpec((tm, tn), lambda i, j: (i, j)),
        out_shape=jax.ShapeDtypeStruct((n, ncols), F32),
        scratch_shapes=[pltpu.VMEM((tm, d), BF16)],
        compiler_params=_cparams(("parallel", "arbitrary"), est),
        name="in_proj",
    )(x2, gains, mod, w_main, w_tail)


GROUP = 4


def _block_diag(x, mask01):
    return jnp.concatenate([x.astype(BF16)] * GROUP, axis=0) * mask01


def _interleave(gens):
    gens = list(gens)
    while gens:
        for gen in list(gens):
            try:
                next(gen)
            except StopIteration:
                gens.remove(gen)


def _dn_kernel(q_ref, k_ref, v_ref, z_ref, ba_ref, cw_ref, alog_ref, dtb_ref, nw_ref, o_ref,
               tail_ref, qs_ref, ks_ref, vs_ref, s_ref, *, tb, heads, hd, conv_k):
    width = heads * hd
    t = pl.program_id(1)

    @pl.when(t == 0)
    def _():
        tail_ref[...] = jnp.zeros_like(tail_ref)
        s_ref[...] = jnp.zeros_like(s_ref)

    for seg, (src, dst) in enumerate(((q_ref, qs_ref), (k_ref, ks_ref), (v_ref, vs_ref))):
        cur = src[...]
        y = _silu(_causal_conv(cur, tail_ref[seg], cw_ref[:, seg * width:(seg + 1) * width], conv_k))
        tail_ref[seg] = cur[tb - V7X_SUBLANES:, :]
        if seg == 2:
            dst[...] = y
        else:
            post = hd ** -0.5 if seg == 0 else 1.0
            for h in range(heads):
                yh = y[:, h * hd:(h + 1) * hd]
                inv = lax.rsqrt(jnp.sum(yh * yh, axis=-1, keepdims=True) + EPS)
                dst[:, h * hd:(h + 1) * hd] = yh * (inv * post) if post != 1.0 else yh * inv

    ba = ba_ref[...]
    lane = lax.broadcasted_iota(jnp.int32, (tb, V7X_LANES), 1)
    beta_all = jax.nn.sigmoid(ba)
    g_all = jnp.where((lane >= heads) & (lane < 2 * heads),
                      -jnp.exp(alog_ref[...]) * _softplus(ba + dtb_ref[...]), 0.0)

    ri = lax.broadcasted_iota(jnp.int32, (tb, tb), 0)
    ci = lax.broadcasted_iota(jnp.int32, (tb, tb), 1)
    same_chunk = _div_pow2(ri, CHUNK) == _div_pow2(ci, CHUNK)
    cum_sel = jnp.where(same_chunk & (ci <= ri), 1.0, 0.0).astype(BF16)
    tot_sel = jnp.where(same_chunk, 1.0, 0.0).astype(BF16)
    gc_all = _dot_sel(cum_sel, g_all)
    gtot_all = _dot_sel(tot_sel, g_all)

    half = V7X_LANES // 2
    step = 2 * heads
    g1, g2, g3 = _split3(gc_all)
    wp = g1.astype(F32) + pltpu.roll(g2.astype(F32), step, 1) + pltpu.roll(g3.astype(F32), 2 * step, 1)
    wq = pltpu.roll(wp, half, 1)
    lu = lane - half - heads
    upper_any = (lu >= 0) & (lu < 3 * step) & ((lu & (step - 1)) < heads)
    u_all = (wp + jnp.where(upper_any, 1.0, 0.0)).astype(BF16)

    n_groups = heads // GROUP
    rows_g = GROUP * CHUNK
    lane_g = lax.broadcasted_iota(jnp.int32, (rows_g, V7X_LANES), 1)
    head_g = _div_pow2(lax.broadcasted_iota(jnp.int32, (rows_g, V7X_LANES), 0), CHUNK)
    pick_lo, pick_hi = [], []
    for g in range(n_groups):
        off = lane_g - heads - g * GROUP - head_g
        pick_lo.append((off == 0) | (off == step) | (off == 2 * step))
        offh = off - half
        pick_hi.append((offh == 0) | (offh == step) | (offh == 2 * step))

    ii = lax.broadcasted_iota(jnp.int32, (CHUNK, GROUP * CHUNK), 0)
    jj = lax.broadcasted_iota(jnp.int32, (CHUNK, GROUP * CHUNK), 1) & (CHUNK - 1)
    lower = ii >= jj
    strict = ii > jj
    eye = jnp.where(ii == jj, 1.0, 0.0).astype(F32)
    i16, j16, i32, j32 = _div_pow2(ii, 16), _div_pow2(jj, 16), _div_pow2(ii, 32), _div_pow2(jj, 32)
    m_diag16 = i16 == j16
    m_off32 = (i32 == j32) & (i16 > j16)
    m_off64 = i32 > j32
    bd_rows = _div_pow2(lax.broadcasted_iota(jnp.int32, (rows_g, GROUP * hd), 0), CHUNK)
    bd_w = jnp.where(_div_pow2(lax.broadcasted_iota(jnp.int32, (rows_g, rows_g), 0), CHUNK)
                     == _div_pow2(lax.broadcasted_iota(jnp.int32, (rows_g, rows_g), 1), CHUNK), 1.0, 0.0).astype(BF16)
    bd_k = jnp.where(bd_rows == _div_pow2(lax.broadcasted_iota(jnp.int32, (rows_g, GROUP * hd), 1), hd),
                     1.0, 0.0).astype(BF16)
    nw = nw_ref[...]
    stash = {}

    def parallel_part(c):
        r = slice(c * CHUNK, (c + 1) * CHUNK)
        qg, kd = [None] * heads, [None] * heads
        a4, rhs, intra = [], [], []
        for g in range(n_groups):
            kbs, qs_, ks_, wrhs = [], [], [], []
            for hh in range(GROUP):
                h = g * GROUP + hh
                hs = slice(h * hd, (h + 1) * hd)
                gcol = gc_all[r, heads + h:heads + h + 1]
                gtot = gtot_all[r, heads + h:heads + h + 1]
                beta = beta_all[r, h:h + 1]
                egc = jnp.exp(gcol)
                q, k, v = qs_ref[r, hs], ks_ref[r, hs], vs_ref[r, hs]
                kb = k * beta
                kbs.append(kb)
                qs_.append(q)
                ks_.append(k)
                wrhs.append(jnp.concatenate([kb * egc, v * beta], axis=1))
                qg[h] = (q * egc).astype(BF16)
                kd[h] = (k * jnp.exp(gtot - gcol)).astype(BF16)
            lhs = jnp.concatenate([jnp.concatenate(kbs, axis=1), jnp.concatenate(qs_, axis=1)], axis=0)
            kq = _dot_nt(lhs.astype(BF16), _block_diag(jnp.concatenate(ks_, axis=1), bd_k))
            wq4 = jnp.concatenate([wq[r]] * GROUP, axis=0)
            vg = jnp.where(pick_hi[g], -wq4, jnp.where(pick_lo[g], 1.0, 0.0)).astype(BF16)
            decay = jnp.exp(jnp.where(lower, _dot_nt(u_all[r], vg), -jnp.inf))
            a4.append(jnp.where(strict, kq[:CHUNK] * decay, 0.0))
            intra.append((kq[CHUNK:] * decay).astype(BF16))
            rhs.append(jnp.concatenate(wrhs, axis=0).astype(BF16))
        yield
        b = [jnp.where(m_diag16, a, 0.0) for a in a4]
        p = [eye - x for x in b]
        b = [_dot(x.astype(BF16), _block_diag(x, bd_w)) for x in b]
        yield
        for _ in range(2):
            pb = [_dot(jnp.concatenate([x, y], axis=0).astype(BF16), _block_diag(y, bd_w)) for x, y in zip(p, b)]
            p = [x + y[:CHUNK] for x, y in zip(p, pb)]
            b = [y[CHUNK:] for y in pb]
            yield
        p = [x + _dot(x.astype(BF16), _block_diag(y, bd_w)) for x, y in zip(p, b)]
        yield
        for m in (m_off32, m_off64):
            tt = [_dot(jnp.where(m, a, 0.0).astype(BF16), _block_diag(x, bd_w)) for a, x in zip(a4, p)]
            yield
            p = [x - _dot(x.astype(BF16), _block_diag(y, bd_w)) for x, y in zip(p, tt)]
            yield
        wu = [_dot(_block_diag(x, bd_w), y) for x, y in zip(p, rhs)]
        w = [wu[h // GROUP][(h % GROUP) * CHUNK:(h % GROUP + 1) * CHUNK, :hd].astype(BF16) for h in range(heads)]
        u = [wu[h // GROUP][(h % GROUP) * CHUNK:(h % GROUP + 1) * CHUNK, hd:] for h in range(heads)]
        r8 = slice(c * CHUNK, c * CHUNK + V7X_SUBLANES)
        gl = jnp.concatenate([jnp.broadcast_to(jnp.exp(gtot_all[r8, heads + h:heads + h + 1]), (V7X_SUBLANES, hd))
                              for h in range(heads)], axis=1)
        stash[c] = (w, u, qg, kd, intra, gl)

    def recurrent_part(c):
        r = slice(c * CHUNK, (c + 1) * CHUNK)
        w, u, qg, kd, intra, gl = stash.pop(c)
        zs = jnp.zeros((hd, hd), BF16)
        zc = jnp.zeros((CHUNK, hd), BF16)
        states, ws = [], []
        for pr in range(heads // 2):
            h0, h1 = 2 * pr, 2 * pr + 1
            s2 = s_ref[pr]
            sb = s2.astype(BF16)
            bds = jnp.concatenate([jnp.concatenate([sb[:, :hd], zs], axis=1),
                                   jnp.concatenate([zs, sb[:, hd:]], axis=1)], axis=0)
            lhs = jnp.concatenate([jnp.concatenate([w[h0], w[h1]], axis=1),
                                   jnp.concatenate([qg[h0], qg[h1]], axis=1)], axis=0)
            states.append(s2)
            ws.append(_dot(lhs, bds))
        yield
        outs = []
        for pr in range(heads // 2):
            h0, h1 = 2 * pr, 2 * pr + 1
            vnb = (jnp.concatenate([u[h0], u[h1]], axis=1) - ws[pr][:CHUNK]).astype(BF16)
            bdv = jnp.concatenate([jnp.concatenate([vnb[:, :hd], zc], axis=1),
                                   jnp.concatenate([zc, vnb[:, hd:]], axis=1)], axis=0)
            g, sub = divmod(pr, GROUP // 2)
            intra2 = intra[g][:, sub * 2 * CHUNK:(sub + 1) * 2 * CHUNK]
            outs.append(ws[pr][CHUNK:] + _dot(intra2, bdv))
            ds = _dot_tn(jnp.concatenate([kd[h0], kd[h1]], axis=0), bdv)
            s_ref[pr] = states[pr] * gl[0:1, pr * 2 * hd:(pr + 1) * 2 * hd] + ds
        yield
        for pr in range(heads // 2):
            for sub in range(2):
                h = 2 * pr + sub
                hs = slice(h * hd, (h + 1) * hd)
                o = outs[pr][:, sub * hd:(sub + 1) * hd]
                on = o * lax.rsqrt(jnp.mean(o * o, axis=-1, keepdims=True) + EPS) * nw
                o_ref[r, hs] = (on * _silu(z_ref[r, hs])).astype(o_ref.dtype)

    def in_order(*gens):
        for gen in gens:
            yield from gen

    n_chunks = tb // CHUNK
    ahead = 4
    _interleave([parallel_part(c) for c in range(min(ahead, n_chunks))])
    for c0 in range(0, n_chunks, ahead):
        nxt = [parallel_part(c) for c in range(c0 + ahead, min(c0 + 2 * ahead, n_chunks))]
        _interleave(nxt + [in_order(*[recurrent_part(c) for c in range(c0, min(c0 + ahead, n_chunks))])])


def _deltanet(proj, conv_w, alog_rows, dtb_rows, norm_w, layer, batch, seq, heads, hd, ba_blk, tb=512):
    width = heads * hd
    conv_k = conv_w.shape[1]
    tb = min(tb, seq)
    assert conv_k - 1 <= V7X_SUBLANES and tb % CHUNK == 0 and seq % tb == 0
    assert heads % GROUP == 0 and 6 * heads + heads <= V7X_LANES // 2
    nt = seq // tb

    def col(cb):
        return lambda b, t: (b * nt + t, cb)

    est = (2 * 4 * tb * width * 4 + 2 * tb * width * 2 + 3 * V7X_SUBLANES * width * 4 + 9 * tb * width * 4
           + heads * hd * hd * 4 + 2 * conv_k * 3 * width * 4 + 8 * tb * tb * 4)
    kern = functools.partial(_dn_kernel, tb=tb, heads=heads, hd=hd, conv_k=conv_k)
    return pl.pallas_call(
        kern,
        grid=(batch, nt),
        in_specs=[
            pl.BlockSpec((tb, width), col(0)),
            pl.BlockSpec((tb, width), col(1)),
            pl.BlockSpec((tb, width), col(2)),
            pl.BlockSpec((tb, width), col(3)),
            pl.BlockSpec((tb, V7X_LANES), col(ba_blk)),
            pl.BlockSpec((None, conv_k, 3 * width), lambda b, t: (layer, 0, 0)),
            pl.BlockSpec((None, 1, V7X_LANES), lambda b, t: (layer, 0, 0)),
            pl.BlockSpec((None, 1, V7X_LANES), lambda b, t: (layer, 0, 0)),
            pl.BlockSpec((None, 1, hd), lambda b, t: (layer, 0, 0)),
        ],
        out_specs=pl.BlockSpec((tb, width), lambda b, t: (b * nt + t, 0)),
        out_shape=jax.ShapeDtypeStruct((batch * seq, width), BF16),
        scratch_shapes=[
            pltpu.VMEM((3, V7X_SUBLANES, width), F32),
            pltpu.VMEM((tb, width), F32),
            pltpu.VMEM((tb, width), F32),
            pltpu.VMEM((tb, width), F32),
            pltpu.VMEM((heads // 2, hd, 2 * hd), F32),
        ],
        compiler_params=_cparams(("parallel", "arbitrary"), est),
        name="gated_deltanet",
    )(proj, proj, proj, proj, proj, conv_w, alog_rows, dtb_rows, norm_w)


def _rope_kernel(f_ref, cos_ref, sin_ref, *, tr):
    pos = lax.broadcasted_iota(jnp.int32, (tr, V7X_LANES), 0) + pl.program_id(0) * tr
    ang = pos.astype(F32) * f_ref[...]
    lane = lax.broadcasted_iota(jnp.int32, (tr, V7X_LANES), 1)
    cos_ref[...] = jnp.cos(ang)
    sin_ref[...] = jnp.where((lane & (AT_HEAD_DIM - 1)) < AT_HEAD_DIM // 2, -jnp.sin(ang), jnp.sin(ang))


def _rope_tables(seq, tr=512):
    half = AT_HEAD_DIM // 2
    inv_freq = ROPE_THETA ** (-jnp.arange(half, dtype=F32) * 2.0 / AT_HEAD_DIM)
    f_row = jnp.tile(inv_freq, V7X_LANES // half).reshape(1, V7X_LANES)
    tr = min(tr, seq)
    return pl.pallas_call(
        functools.partial(_rope_kernel, tr=tr),
        grid=(seq // tr,),
        in_specs=[pl.BlockSpec((1, V7X_LANES), lambda i: (0, 0))],
        out_specs=[pl.BlockSpec((tr, V7X_LANES), lambda i: (i, 0))] * 2,
        out_shape=[jax.ShapeDtypeStruct((seq, V7X_LANES), F32)] * 2,
        name="rope_tables",
    )(f_row)


def _attn_kernel(sink_ref, q_ref, kc_ref, kp_ref, vc_ref, vp_ref, cc_ref, sc_ref, cp_ref, sp_ref,
                 o_ref, *, pairs, layer, tq):
    n = pl.program_id(1)
    w = WINDOW
    half = V7X_LANES // 2
    quarter = half // 2
    cos_c, sin_c = cc_ref[...], sc_ref[...]

    def rope(x, cos, sin):
        ln = lax.broadcasted_iota(jnp.int32, x.shape, 1)
        partner = jnp.where((ln & (half - 1)) < quarter,
                            pltpu.roll(x, V7X_LANES - quarter, 1), pltpu.roll(x, quarter, 1))
        return x * cos + partner * sin

    lane = lax.broadcasted_iota(jnp.int32, (w + tq, V7X_LANES), 1)
    first = lane < half
    kcat = jnp.concatenate([rope(kp_ref[...], cp_ref[...], sp_ref[...]), rope(kc_ref[...], cos_c, sin_c)], axis=0)
    vcat = jnp.concatenate([vp_ref[...], vc_ref[...]], axis=0)
    kswap, vswap = pltpu.roll(kcat, half, 1), pltpu.roll(vcat, half, 1)
    ones_lo, ones_hi = jnp.where(first, 1.0, 0.0), jnp.where(first, 0.0, 1.0)

    def placed(x, xswap, g, s):
        src = x if g == s else xswap
        return jnp.where(first, src, 0.0) if s == 0 else jnp.where(first, 0.0, src)

    k_at = [[placed(kcat, kswap, g, s).astype(BF16) for s in range(2)] for g in range(AT_KV_HEADS)]
    v_at = [[jnp.concatenate([placed(vcat, vswap, g, s), ones_lo if s == 0 else ones_hi], axis=1).astype(BF16)
             for s in range(2)] for g in range(AT_KV_HEADS)]

    r = lax.broadcasted_iota(jnp.int32, (w, 2 * w), 0)
    j = lax.broadcasted_iota(jnp.int32, (w, 2 * w), 1)
    in_band = (j > r) & (j <= r + w)
    out_first = lax.broadcasted_iota(jnp.int32, (w, V7X_LANES), 1) < half
    scale = AT_HEAD_DIM ** -0.5 * LOG2E
    cos_q, sin_q = cos_c * scale, sin_c * scale

    gp = pairs // AT_KV_HEADS
    visible_any = jnp.concatenate([in_band] * pairs, axis=0)
    visible_first = jnp.concatenate([in_band & ((n > 0) | (j >= w))] * pairs, axis=0)
    out_first_all = jnp.concatenate([out_first] * pairs, axis=0)
    sink_a = jnp.concatenate([jnp.full((w, V7X_LANES), sink_ref[layer, 2 * p] * LOG2E, F32) for p in range(pairs)], axis=0)
    sink_b = jnp.concatenate([jnp.full((w, V7X_LANES), sink_ref[layer, 2 * p + 1] * LOG2E, F32) for p in range(pairs)], axis=0)

    def row_max(x, sink):
        return jnp.maximum(jnp.broadcast_to(jnp.max(x, axis=-1, keepdims=True), sink.shape), sink)

    def scores(sub):
        rows = slice(sub * w, (sub + 1) * w)
        keys = slice(sub * w, (sub + 2) * w)
        zq = jnp.zeros((w, V7X_LANES), BF16)
        qs = []
        for p in range(pairs):
            qr = rope(q_ref[rows, p * V7X_LANES:(p + 1) * V7X_LANES], cos_q[rows], sin_q[rows]).astype(BF16)
            qs.append(jnp.concatenate([qr, zq] if p < gp else [zq, qr], axis=1))
        kk = jnp.concatenate([jnp.concatenate([k_at[g][0][keys], k_at[g][1][keys]], axis=0)
                              for g in range(AT_KV_HEADS)], axis=1)
        return _dot_nt(jnp.concatenate(qs, axis=0), kk)

    n_sub = tq // w
    s_next = scores(0)
    for sub in range(n_sub):
        s = s_next
        if sub + 1 < n_sub:
            s_next = scores(sub + 1)
        rows = slice(sub * w, (sub + 1) * w)
        keys = slice(sub * w, (sub + 2) * w)
        visible = visible_first if sub == 0 else visible_any
        sa = jnp.where(visible, s[:, :2 * w], -jnp.inf)
        sb = jnp.where(visible, s[:, 2 * w:], -jnp.inf)
        ma = row_max(sa, sink_a)
        mb = row_max(sb, sink_b)
        pr = jnp.concatenate([jnp.exp2(sa - jnp.concatenate([ma, ma], axis=1)),
                              jnp.exp2(sb - jnp.concatenate([mb, mb], axis=1))], axis=1).astype(BF16)
        ov = jnp.concatenate(
            [_dot(pr[g * gp * w:(g + 1) * gp * w], jnp.concatenate([v_at[g][0][keys], v_at[g][1][keys]], axis=0))
             for g in range(AT_KV_HEADS)], axis=0)
        den = ov[:, V7X_LANES:] + jnp.where(out_first_all, jnp.exp2(sink_a - ma), jnp.exp2(sink_b - mb))
        out = (ov[:, :V7X_LANES] / den).astype(o_ref.dtype)
        for p in range(pairs):
            o_ref[rows, p * V7X_LANES:(p + 1) * V7X_LANES] = out[p * w:(p + 1) * w]


def _attention(proj, sinks, cos_t, sin_t, layer, batch, seq, dn_width, at_width, tq=512):
    w = WINDOW
    tq = min(tq, seq)
    assert tq % w == 0 and seq % tq == 0
    nb = seq // tq
    wpb = tq // w
    pairs = at_width // V7X_LANES
    q_blk = (4 * dn_width) // at_width
    k_blk = (4 * dn_width + at_width) // V7X_LANES

    def cur(cb):
        return lambda b, n: (b * nb + n, cb)

    def prev(cb):
        return lambda b, n: (b * nb * wpb + jnp.maximum(n * wpb - 1, 0), cb)

    def prev_t(b, n):
        return (jnp.maximum(n * wpb - 1, 0), 0)

    est = 2 * tq * at_width * 4 + 2 * tq * at_width * 2 + 24 * (tq + w) * V7X_LANES * 4 + 64 * w * w * 4
    return pl.pallas_call(
        functools.partial(_attn_kernel, pairs=pairs, layer=layer, tq=tq),
        grid=(batch, nb),
        in_specs=[
            pl.BlockSpec(memory_space=pltpu.SMEM),
            pl.BlockSpec((tq, at_width), cur(q_blk)),
            pl.BlockSpec((tq, V7X_LANES), cur(k_blk)),
            pl.BlockSpec((w, V7X_LANES), prev(k_blk)),
            pl.BlockSpec((tq, V7X_LANES), cur(k_blk + 1)),
            pl.BlockSpec((w, V7X_LANES), prev(k_blk + 1)),
            pl.BlockSpec((tq, V7X_LANES), lambda b, n: (n, 0)),
            pl.BlockSpec((tq, V7X_LANES), lambda b, n: (n, 0)),
            pl.BlockSpec((w, V7X_LANES), prev_t),
            pl.BlockSpec((w, V7X_LANES), prev_t),
        ],
        out_specs=pl.BlockSpec((tq, at_width), lambda b, n: (b * nb + n, 0)),
        out_shape=jax.ShapeDtypeStruct((batch * seq, at_width), BF16),
        compiler_params=_cparams(("parallel", "arbitrary"), est),
        name="swa_sink_attention",
    )(sinks, proj, proj, proj, proj, proj, cos_t, sin_t, cos_t, sin_t)


def _out_proj_kernel(dn_ref, at_ref, w_ref, x_ref, mod_ref, o_ref, *, dn_width):
    acc = _dot(dn_ref[...], w_ref[:dn_width, :]) + _dot(at_ref[...], w_ref[dn_width:, :])
    o_ref[...] = x_ref[...] + mod_ref[2:3, :] * acc


def _out_proj(dn, at, w_bf16, x2, mod, layer, tokens_per_batch, tm=512):
    n, d = x2.shape
    dnw, atw = dn.shape[1], at.shape[1]
    tm = min(tm, tokens_per_batch)
    tiles_per_batch = tokens_per_batch // tm
    est = 2 * (dnw + atw) * d * 2 + 2 * tm * (dnw + atw) * 2 + 4 * tm * d * 4 + tm * d * 4
    return pl.pallas_call(
        functools.partial(_out_proj_kernel, dn_width=dnw),
        grid=(n // tm,),
        in_specs=[
            pl.BlockSpec((tm, dnw), lambda i: (i, 0)),
            pl.BlockSpec((tm, atw), lambda i: (i, 0)),
            pl.BlockSpec((None, dnw + atw, d), lambda i: (layer, 0, 0)),
            pl.BlockSpec((tm, d), lambda i: (i, 0)),
            pl.BlockSpec((None, None, N_MOD, d), lambda i: (layer, i // tiles_per_batch, 0, 0)),
        ],
        out_specs=pl.BlockSpec((tm, d), lambda i: (i, 0)),
        out_shape=jax.ShapeDtypeStruct((n, d), F32),
        compiler_params=_cparams(("parallel",), est),
        name="out_proj",
    )(dn, at, w_bf16, x2, mod)


def _ffn_kernel(x_ref, g_ref, mod_ref, wg_ref, wu_ref, wd_ref, fg_ref, o_ref, h_ref, acc_ref, *, final_norm):
    f = pl.program_id(1)

    @pl.when(f == 0)
    def _():
        _norm_mod_rows(x_ref, h_ref, g_ref[...], mod_ref[3:4, :], mod_ref[4:5, :])
        acc_ref[...] = jnp.zeros_like(acc_ref)

    h = h_ref[...]
    gate = _dot(h, wg_ref[...])
    up = _dot(h, wu_ref[...])
    acc_ref[...] += _dot((_silu(gate) * up).astype(BF16), wd_ref[...])

    @pl.when(f == pl.num_programs(1) - 1)
    def _():
        gate_f = mod_ref[5:6, :]
        final_gain = fg_ref[...]

        def body(i, carry):
            r = pl.ds(pl.multiple_of(i * NORM_ROWS, NORM_ROWS), NORM_ROWS)
            y = x_ref[r, :] + gate_f * acc_ref[r, :]
            if final_norm:
                y = y * lax.rsqrt(jnp.mean(y * y, axis=-1, keepdims=True) + EPS) * final_gain
            o_ref[r, :] = y
            return carry

        lax.fori_loop(0, x_ref.shape[0] // NORM_ROWS, body, 0)


def _ffn(x2, gains, mod, w_gu_bf16, w_dn_bf16, final_gain, layer, tokens_per_batch, final_norm, tm=512, tf=512):
    n, d = x2.shape
    ffn = w_dn_bf16.shape[1]
    tm = min(tm, tokens_per_batch)
    tiles_per_batch = tokens_per_batch // tm
    nf = ffn // tf
    est = 4 * tm * d * 4 + tm * d * 4 + tm * d * 2 + 6 * d * tf * 2 + 4 * tm * tf * 4
    return pl.pallas_call(
        functools.partial(_ffn_kernel, final_norm=final_norm),
        grid=(n // tm, nf),
        in_specs=[
            pl.BlockSpec((tm, d), lambda i, f: (i, 0)),
            pl.BlockSpec((None, 1, d), lambda i, f: (layer, 0, 0)),
            pl.BlockSpec((None, None, N_MOD, d), lambda i, f: (layer, i // tiles_per_batch, 0, 0)),
            pl.BlockSpec((None, d, tf), lambda i, f: (layer, 0, f)),
            pl.BlockSpec((None, d, tf), lambda i, f: (layer, 0, nf + f)),
            pl.BlockSpec((None, tf, d), lambda i, f: (layer, f, 0)),
            pl.BlockSpec((1, d), lambda i, f: (0, 0)),
        ],
        out_specs=pl.BlockSpec((tm, d), lambda i, f: (i, 0)),
        out_shape=jax.ShapeDtypeStruct((n, d), F32),
        scratch_shapes=[pltpu.VMEM((tm, d), BF16), pltpu.VMEM((tm, d), F32)],
        compiler_params=_cparams(("parallel", "arbitrary"), est),
        name="swiglu_ffn",
    )(x2, gains, mod, w_gu_bf16, w_gu_bf16, w_dn_bf16, final_gain)


def kernel(x, c, ln_mix, ln_ffn, w_ada, b_ada, w_in, dn_conv_w, dn_a_log, dn_dt_bias, dn_norm_w,
           attn_sinks, w_out, w_gate_up, w_down, ln_final):
    batch, seq, d = x.shape
    depth = w_in.shape[0]
    dn_heads = dn_a_log.shape[1]
    dn_hd = dn_norm_w.shape[1]
    dn_width = dn_heads * dn_hd
    at_q_heads = attn_sinks.shape[1]
    at_width = at_q_heads * AT_HEAD_DIM
    kv_width = AT_KV_HEADS * AT_HEAD_DIM
    assert kv_width == V7X_LANES and at_width % V7X_LANES == 0 and seq % WINDOW == 0
    assert w_in.shape[2] == 4 * dn_width + 2 * dn_heads + at_width + 2 * kv_width
    assert 2 * dn_heads <= V7X_LANES

    o_b = 4 * dn_width
    o_q = o_b + 2 * dn_heads
    used = w_in.shape[2]
    in_tile = 512
    ncols = -(-(used - 2 * dn_heads + V7X_LANES) // in_tile) * in_tile
    w_in_t = jnp.swapaxes(w_in, 1, 2).astype(BF16)
    w_in_tail = jnp.concatenate([
        w_in_t[:, o_q:],
        w_in_t[:, o_b:o_q],
        jnp.zeros((depth, ncols - used, d), BF16),
    ], axis=1)
    w_out_p = w_out.astype(BF16)
    w_gu = w_gate_up.astype(BF16)
    w_dn = w_down.astype(BF16)

    lanes_pad = V7X_LANES - 2 * dn_heads
    alog_rows = jnp.pad(dn_a_log, ((0, 0), (dn_heads, lanes_pad))).reshape(depth, 1, V7X_LANES)
    dtb_rows = jnp.pad(dn_dt_bias, ((0, 0), (dn_heads, lanes_pad))).reshape(depth, 1, V7X_LANES)
    norm_rows = dn_norm_w.reshape(depth, 1, dn_hd)
    gains_mix = ln_mix.reshape(depth, 1, d)
    gains_ffn = ln_ffn.reshape(depth, 1, d)
    final_gain = ln_final.reshape(1, d)
    ba_blk = (used - 2 * dn_heads) // V7X_LANES

    c_pad = jnp.pad(c, ((0, V7X_SUBLANES - batch % V7X_SUBLANES), (0, 0))) if batch % V7X_SUBLANES else c
    mod = _adaln(c_pad, w_ada, b_ada)[:, :batch].reshape(depth, batch, N_MOD, d)
    cos_t, sin_t = _rope_tables(seq)

    x2 = x.reshape(batch * seq, d)
    for l in range(depth):
        proj = _in_proj(x2, gains_mix, mod, w_in_t, w_in_tail, o_b, l, seq)
        dn = _deltanet(proj, dn_conv_w, alog_rows, dtb_rows, norm_rows, l, batch, seq, dn_heads, dn_hd, ba_blk)
        at = _attention(proj, attn_sinks, cos_t, sin_t, l, batch, seq, dn_width, at_width)
        x2 = _out_proj(dn, at, w_out_p, x2, mod, l, seq)
        x2 = _ffn(x2, gains_ffn, mod, w_gu, w_dn, final_gain, l, seq, final_norm=(l == depth - 1))
    return x2.reshape(batch, seq, d)
```
